```python
import math
import jax
import jax.numpy as jnp
from jax import lax
import numpy as np

D_MODEL = 1024
BATCH = 8
SEQ = 2048
DEPTH = 2
DEC_BATCH = 128
DEC_SEQ = 1
PAST_LEN = 16384
PAGE_SIZE = 128

N_META = 16
CHUNK = 64
A_HEADS = 4
A_DK = 128
A_DV = 128
A_QK = A_HEADS * A_DK
A_V = A_HEADS * A_DV
CONV_W = 4
CONV_CH = 2 * A_QK + A_V
B_HEADS = 4
B_DK = 64
B_DV = 128
B_QK = B_HEADS * B_DK
B_V = B_HEADS * B_DV
B_GATE_RANK = 16
B_GATE_TAU = 16.0
D_FF = ((8 * D_MODEL + 3 * 256 - 1) // (3 * 256)) * 256
IN_SIZES = (CONV_CH, A_V, A_HEADS, A_HEADS, B_QK, B_QK, B_V, B_V, B_GATE_RANK, D_MODEL, D_MODEL)
D_IN = sum(IN_SIZES)
DEEPNORM_ALPHA = (2.0 * DEPTH) ** 0.25
DEEPNORM_BETA = (8.0 * DEPTH) ** -0.25
F32 = jnp.float32

kernel_name = 'hybrid_gdn_gla_deepnorm_step'


def _layer_norm(x, g, b, eps=1e-5):
    xf = x.astype(F32)
    mu = jnp.mean(xf, axis=-1, keepdims=True)
    var = jnp.mean(jnp.square(xf - mu), axis=-1, keepdims=True)
    return ((xf - mu) * lax.rsqrt(var + eps) * g.astype(F32) + b.astype(F32)).astype(x.dtype)


def _rms_norm(x, w, eps=1e-6):
    xf = x.astype(F32)
    return xf * lax.rsqrt(jnp.mean(xf * xf, axis=-1, keepdims=True) + eps) * w.astype(F32)


def _l2norm(x, eps=1e-6):
    return x * lax.rsqrt(jnp.sum(x * x, axis=-1, keepdims=True) + eps)


def _heads(t, n):
    b, l = t.shape[:2]
    return t.reshape(b, l, n, -1).transpose(0, 2, 1, 3).astype(F32)


def _split_in(p):
    offsets = np.cumsum(IN_SIZES)[:-1].tolist()
    return jnp.split(p, offsets, axis=-1)


def _causal_conv(u, buf, w):
    l = u.shape[1]
    ext = jnp.concatenate([buf.astype(u.dtype), u], axis=1)
    out = sum(ext[:, i:i + l] * w[i] for i in range(CONV_W))
    return jax.nn.silu(out), ext[:, l:]


def _gdn_chunks(q, k, v, g, beta, s0, chunk):
    b, h, l, dk = q.shape
    dv = v.shape[-1]
    n = l // chunk
    r = lambda t: t.reshape(t.shape[:2] + (n, chunk) + t.shape[3:])
    q, k, v, g, beta = r(q), r(k), r(v), r(g), r(beta)
    gam = jnp.cumsum(g, axis=-1)
    incl = jnp.tril(jnp.ones((chunk, chunk), dtype=bool))
    strict = jnp.tril(jnp.ones((chunk, chunk), dtype=bool), k=-1)
    diff = gam[..., :, None] - gam[..., None, :]
    dec_incl = jnp.exp(jnp.where(incl, diff, -jnp.inf))
    dec_strict = jnp.where(strict, dec_incl, 0.0)
    a_mat = beta[..., None] * dec_strict * jnp.einsum('bhnck,bhnjk->bhncj', k, k)
    eye = jnp.eye(chunk, dtype=F32)
    rhs = jnp.concatenate([beta[..., None] * v, (beta * jnp.exp(gam))[..., None] * k], axis=-1)
    sol = lax.linalg.triangular_solve(eye + a_mat, rhs, left_side=True, lower=True, unit_diagonal=True)
    u0, w = sol[..., :dv], sol[..., dv:]
    qk = jnp.einsum('bhnck,bhnjk->bhncj', q, k) * dec_incl
    q_dec = q * jnp.exp(gam)[..., None]
    k_dec = k * jnp.exp(gam[..., -1:] - gam)[..., None]
    g_tot = jnp.exp(gam[..., -1])
    xs = tuple(jnp.moveaxis(t, 2, 0) for t in (u0, w, qk, q_dec, k_dec, g_tot))

    def step(s, inp):
        u0_c, w_c, qk_c, qd_c, kd_c, gt_c = inp
        u = u0_c - jnp.einsum('bhck,bhkv->bhcv', w_c, s)
        o = jnp.einsum('bhck,bhkv->bhcv', qd_c, s) + jnp.einsum('bhcj,bhjv->bhcv', qk_c, u)
        s = gt_c[..., None, None] * s + jnp.einsum('bhck,bhcv->bhkv', kd_c, u)
        return s, o

    s, o = lax.scan(step, s0.astype(F32), xs)
    return jnp.moveaxis(o, 0, 2).reshape(b, h, l, dv), s


def _gla_chunks(q, k, v, lg, s0, chunk):
    b, h, l, dk = q.shape
    dv = v.shape[-1]
    n = l // chunk
    r = lambda t: t.reshape(t.shape[:2] + (n, chunk) + t.shape[3:])
    q, k, v, lg = r(q), r(k), r(v), r(lg)
    cb = jnp.cumsum(lg, axis=-2)
    q_dec = q * jnp.exp(cb)
    k_dec = k * jnp.exp(cb[..., -1:, :] - cb)
    a_tot = jnp.exp(cb[..., -1, :])
    incl = jnp.tril(jnp.ones((chunk, chunk), dtype=bool))[:, :, None]
    xs = tuple(jnp.moveaxis(t, 2, 0) for t in (q, k, v, cb, q_dec, k_dec, a_tot))

    def step(s, inp):
        q_c, k_c, v_c, cb_c, qd_c, kd_c, at_c = inp
        diff = cb_c[:, :, :, None, :] - cb_c[:, :, None, :, :]
        dec = jnp.exp(jnp.where(incl, diff, -jnp.inf))
        attn = jnp.einsum('bhtk,bhjk,bhtjk->bhtj', q_c, k_c, dec)
        o = jnp.einsum('bhck,bhkv->bhcv', qd_c, s) + jnp.einsum('bhtj,bhjv->bhtv', attn, v_c)
        s = at_c[..., None] * s + jnp.einsum('bhck,bhcv->bhkv', kd_c, v_c)
        return s, o

    s, o = lax.scan(step, s0.astype(F32), xs)
    return jnp.moveaxis(o, 0, 2).reshape(b, h, l, dv), s


def _token_mix(h, conv_buf, s_gdn, s_gla, segments, w_in, conv_w, a_log, dt_bias, gdn_norm_w,
               gla_gate_w2, gla_gate_b, gla_norm_w, w_branch_a, w_branch_b, w_out):
    b, l, _ = h.shape
    qkv_a, z_a, beta_a, a_a, q_b, k_b, v_b, r_b, glr_b, gate_a, gate_b = _split_in(h @ w_in)
    qkv_a, conv_buf = _causal_conv(qkv_a, conv_buf, conv_w)
    q_a, k_a, v_a = jnp.split(qkv_a, [A_QK, 2 * A_QK], axis=-1)
    q_a = _l2norm(_heads(q_a, A_HEADS)) * (A_DK ** -0.5)
    k_a = _l2norm(_heads(k_a, A_HEADS))
    v_a = _heads(v_a, A_HEADS)
    beta = jax.nn.sigmoid(beta_a.astype(F32)).transpose(0, 2, 1)
    g = -(jnp.exp(a_log.astype(F32)) * jax.nn.softplus(a_a.astype(F32) + dt_bias.astype(F32)))
    g = g.transpose(0, 2, 1)
    q_b = _heads(q_b, B_HEADS) * (B_DK ** -0.5)
    k_b = _heads(k_b, B_HEADS)
    v_b = _heads(v_b, B_HEADS)
    lg = jax.nn.log_sigmoid((glr_b @ gla_gate_w2 + gla_gate_b).astype(F32)) / B_GATE_TAU
    lg = _heads(lg, B_HEADS)
    o_a, o_b = [], []
    start = 0
    for length, chunk in segments:
        sl = slice(start, start + length)
        oa, s_gdn = _gdn_chunks(q_a[:, :, sl], k_a[:, :, sl], v_a[:, :, sl], g[:, :, sl], beta[:, :, sl], s_gdn, chunk)
        ob, s_gla = _gla_chunks(q_b[:, :, sl], k_b[:, :, sl], lg[:, :, sl], v_b[:, :, sl], s_gla, chunk) if False else _gla_chunks(q_b[:, :, sl], k_b[:, :, sl], v_b[:, :, sl], lg[:, :, sl], s_gla, chunk)
        o_a.append(oa)
        o_b.append(ob)
        start += length
    o_a = jnp.concatenate(o_a, axis=2).transpose(0, 2, 1, 3)
    o_b = jnp.concatenate(o_b, axis=2).transpose(0, 2, 1, 3)
    o_a = _rms_norm(o_a, gdn_norm_w).reshape(b, l, A_V) * jax.nn.silu(z_a.astype(F32))
    o_b = _rms_norm(o_b, gla_norm_w).reshape(b, l, B_V) * jax.nn.silu(r_b.astype(F32))
    y_a = o_a.astype(h.dtype) @ w_branch_a
    y_b = o_b.astype(h.dtype) @ w_branch_b
    merged = jax.nn.sigmoid(gate_a) * y_a + jax.nn.sigmoid(gate_b) * y_b
    return merged @ w_out, conv_buf, s_gdn, s_gla


def _swiglu(x, w_ffn_in, w_ffn_out):
    a, u = jnp.split(x @ w_ffn_in, 2, axis=-1)
    return (jax.nn.silu(a) * u) @ w_ffn_out


def _layer(x, conv_buf, s_gdn, s_gla, segments, w_in, conv_w, a_log, dt_bias, gdn_norm_w, gla_gate_w2,
           gla_gate_b, gla_norm_w, w_branch_a, w_branch_b, w_out, ln1_g, ln1_b, ln2_g, ln2_b, w_ffn_in, w_ffn_out):
    mix, conv_buf, s_gdn, s_gla = _token_mix(x, conv_buf, s_gdn, s_gla, segments, w_in, conv_w, a_log, dt_bias,
                                             gdn_norm_w, gla_gate_w2, gla_gate_b, gla_norm_w,
                                             w_branch_a, w_branch_b, w_out)
    x = _layer_norm(DEEPNORM_ALPHA * x + mix, ln1_g, ln1_b)
    x = _layer_norm(DEEPNORM_ALPHA * x + _swiglu(x, w_ffn_in, w_ffn_out), ln2_g, ln2_b)
    return x, s_gdn.astype(x.dtype), s_gla.astype(x.dtype), conv_buf


def setup_inputs(seed: int = 0) -> dict:
    key = jax.random.key(seed)
    ks = jax.random.split(key, 24)
    nrm = lambda k, shape, s: jax.random.normal(k, shape, F32) * s
    dt = jnp.exp(jax.random.uniform(ks[5], (DEPTH, A_HEADS), F32, math.log(1e-3), math.log(1e-1)))
    return {
        'x_prompt': nrm(ks[0], (BATCH, SEQ, D_MODEL), 1.0),
        'x_sample': nrm(ks[1], (DEC_BATCH, DEC_SEQ, D_MODEL), 1.0),
        'state_gdn': nrm(ks[2], (DEPTH, DEC_BATCH, A_HEADS, A_DK, A_DV), 0.3),
        'state_gla': nrm(ks[3], (DEPTH, DEC_BATCH, B_HEADS, B_DK, B_DV), 0.3),
        'state_conv': nrm(ks[4], (DEPTH, DEC_BATCH, CONV_W - 1, CONV_CH), 1.0),
        'meta_tokens': nrm(ks[6], (N_META, D_MODEL), 1.0),
        'w_in': nrm(ks[7], (DEPTH, D_MODEL, D_IN), D_MODEL ** -0.5),
        'conv_w': nrm(ks[8], (DEPTH, CONV_W, CONV_CH), CONV_W ** -0.5),
        'a_log': jnp.log(jax.random.uniform(ks[9], (DEPTH, A_HEADS), F32, 1.0, 16.0)),
        'dt_bias': dt + jnp.log(-jnp.expm1(-dt)),
        'gdn_norm_w': 1.0 + nrm(ks[10], (DEPTH, A_DV), 0.02),
        'gla_gate_w2': nrm(ks[11], (DEPTH, B_GATE_RANK, B_QK), B_GATE_RANK ** -0.5),
        'gla_gate_b': nrm(ks[12], (DEPTH, B_QK), 0.02),
        'gla_norm_w': 1.0 + nrm(ks[13], (DEPTH, B_DV), 0.02),
        'w_branch_a': nrm(ks[14], (DEPTH, A_V, D_MODEL), A_V ** -0.5 * DEEPNORM_BETA),
        'w_branch_b': nrm(ks[15], (DEPTH, B_V, D_MODEL), B_V ** -0.5 * DEEPNORM_BETA),
        'w_out': nrm(ks[16], (DEPTH, D_MODEL, D_MODEL), D_MODEL ** -0.5 * DEEPNORM_BETA),
        'ln1_g': 1.0 + nrm(ks[17], (DEPTH, D_MODEL), 0.02),
        'ln1_b': nrm(ks[18], (DEPTH, D_MODEL), 0.02),
        'ln2_g': 1.0 + nrm(ks[19], (DEPTH, D_MODEL), 0.02),
        'ln2_b': nrm(ks[20], (DEPTH, D_MODEL), 0.02),
        'w_ffn_in': nrm(ks[21], (DEPTH, D_MODEL, 2 * D_FF), D_MODEL ** -0.5),
        'w_ffn_out': nrm(ks[22], (DEPTH, D_FF, D_MODEL), D_FF ** -0.5 * DEEPNORM_BETA),
    }


def reference(x_prompt, x_sample, state_gdn, state_gla, state_conv, meta_tokens, w_in, conv_w, a_log, dt_bias,
              gdn_norm_w, gla_gate_w2, gla_gate_b, gla_norm_w, w_branch_a, w_branch_b, w_out,
              ln1_g, ln1_b, ln2_g, ln2_b, w_ffn_in, w_ffn_out):
    dtp = x_prompt.dtype
    bp = x_prompt.shape[0]
    seq = x_prompt.shape[1]
    dec_seq = x_sample.shape[1]
    meta = jnp.broadcast_to(meta_tokens.astype(dtp)[None], (bp, N_META, D_MODEL))
    hp = jnp.concatenate([meta, x_prompt], axis=1)
    hs = x_sample
    seg_p = ((N_META, N_META), (seq, math.gcd(seq, CHUNK)))
    seg_s = ((dec_seq, math.gcd(dec_seq, CHUNK)),)
    gdn_p, gla_p, conv_p, gdn_s, gla_s, conv_s = [], [], [], [], [], []
    for l in range(DEPTH):
        p = (w_in[l], conv_w[l], a_log[l], dt_bias[l], gdn_norm_w[l], gla_gate_w2[l], gla_gate_b[l], gla_norm_w[l],
             w_branch_a[l], w_branch_b[l], w_out[l], ln1_g[l], ln1_b[l], ln2_g[l], ln2_b[l], w_ffn_in[l], w_ffn_out[l])
        hp, sg, sl, cb = _layer(hp, jnp.zeros((bp, CONV_W - 1, CONV_CH), dtp),
                                jnp.zeros((bp, A_HEADS, A_DK, A_DV), F32),
                                jnp.zeros((bp, B_HEADS, B_DK, B_DV), F32), seg_p, *p)
        gdn_p.append(sg)
        gla_p.append(sl)
        conv_p.append(cb)
        hs, sg, sl, cb = _layer(hs, state_conv[l], state_gdn[l], state_gla[l], seg_s, *p)
        gdn_s.append(sg)
        gla_s.append(sl)
        conv_s.append(cb)
    y_prompt = hp[:, N_META:]
    return (y_prompt, hs, jnp.stack(gdn_p), jnp.stack(gla_p), jnp.stack(conv_p),
            jnp.stack(gdn_s), jnp.stack(gla_s), jnp.stack(conv_s))
```

```python
import functools

import numpy as np
import jax
import jax.numpy as jnp
from jax import lax
from jax.experimental import pallas as pl
from jax.experimental.pallas import tpu as pltpu

F32 = jnp.float32
BF16 = jnp.bfloat16

D_MODEL = 1024
N_META = 16
CHUNK = 64
HEADS = 4
A_DK = 128
A_DV = 128
A_QK = HEADS * A_DK
A_V = HEADS * A_DV
CONV_W = 4
CONV_CH = 2 * A_QK + A_V
B_DK = 64
B_DV = 128
B_QK = HEADS * B_DK
B_V = HEADS * B_DV
B_GATE_RANK = 16
B_GATE_TAU = 16.0
SUB = 16
LANES = 128
SUBLANES = 8
SMALL_W = LANES
SCAN_W = CONV_CH + 2 * B_QK + B_V + SMALL_W
GATE_W = A_V + B_V + 2 * D_MODEL
VMEM_LIMIT = 56 * 1024 * 1024

_BETA0, _G0, _GAM0, _GLR0 = 0, 4, 8, 12


def _dot(a, b):
    return jnp.dot(a, b, preferred_element_type=F32)


def _dot_nt(a, b):
    return lax.dot_general(a, b, (((1,), (1,)), ((), ())), preferred_element_type=F32)


def _dot_tn(a, b):
    return lax.dot_general(a, b, (((0,), (0,)), ((), ())), preferred_element_type=F32)


def _bf(x):
    return x.astype(BF16)


def _split2(x):
    hi = x.astype(BF16)
    lo = (x - hi.astype(F32)).astype(BF16)
    return hi, lo


def _mm3(a, b):
    ah, al = _split2(a)
    bh, bl = _split2(b)
    return _dot(ah, bh) + _dot(ah, bl) + _dot(al, bh)


def _tri_cumsum(ltri, x):
    hi = x.astype(BF16)
    r1 = x - hi.astype(F32)
    mid = r1.astype(BF16)
    lo = (r1 - mid.astype(F32)).astype(BF16)
    return _dot(ltri, hi) + _dot(ltri, mid) + _dot(ltri, lo)


def _sigmoid(x):
    return 1.0 / (1.0 + jnp.exp(-x))


def _silu(x):
    return x * _sigmoid(x)


def _softplus(x):
    return jnp.maximum(x, 0.0) + jnp.log1p(jnp.exp(-jnp.abs(x)))


def _log_sigmoid(x):
    return jnp.minimum(x, 0.0) - jnp.log1p(jnp.exp(-jnp.abs(x)))


def _layer_norm(h, g, b):
    mu = jnp.mean(h, axis=-1, keepdims=True)
    d = h - mu
    var = jnp.mean(d * d, axis=-1, keepdims=True)
    return d * lax.rsqrt(var + 1e-5) * g + b


def _proj_tail(p, act, hp, w2, gb):
    qs, ks = [], []
    for h in range(HEADS):
        qh = act[:, h * A_DK:(h + 1) * A_DK]
        qs.append(qh * lax.rsqrt(jnp.sum(qh * qh, axis=-1, keepdims=True) + 1e-6) * (A_DK ** -0.5))
        kh = act[:, A_QK + h * A_DK:A_QK + (h + 1) * A_DK]
        ks.append(kh * lax.rsqrt(jnp.sum(kh * kh, axis=-1, keepdims=True) + 1e-6))
    qa = jnp.concatenate(qs, axis=-1)
    ka = jnp.concatenate(ks, axis=-1)
    va = act[:, 2 * A_QK:]
    qb = p[:, CONV_CH:CONV_CH + B_QK] * (B_DK ** -0.5)
    kb = p[:, CONV_CH + B_QK:CONV_CH + 2 * B_QK]
    vb = p[:, CONV_CH + 2 * B_QK:CONV_CH + 2 * B_QK + B_V]
    small = p[:, CONV_CH + 2 * B_QK + B_V:]
    beta = _sigmoid(small)
    g = -(jnp.exp(hp[0:1, :]) * _softplus(small + hp[1:2, :]))
    lg = _log_sigmoid(_dot(_bf(small), w2) + gb) * (1.0 / B_GATE_TAU)
    return qa, ka, va, qb, kb, vb, beta, g, lg


def _dup_heads(x):
    parts = []
    for h in range(HEADS):
        xh = x[:, h * B_DK:(h + 1) * B_DK]
        parts += [xh, xh]
    return jnp.concatenate(parts, axis=-1)


def _proj_prompt_kernel(x_ref, carry0_ref, w_ref, cw_ref, hp_ref, w2_ref, gb_ref, ltri_ref,
                        qa_ref, ka_ref, va_ref, q2_ref, k2_ref, vb_ref, cb2_ref, aux_ref, tail_ref,
                        carry_ref, *, tm, n_null):
    j = pl.program_id(1)

    @pl.when(j == 0)
    def _():
        carry_ref[...] = carry0_ref[...]

    p = _dot(_bf(x_ref[0]), w_ref[...])
    u = p[:, :CONV_CH]
    prev8 = carry_ref[...]
    cw = cw_ref[...]
    row8 = lax.broadcasted_iota(jnp.int32, (SUBLANES, CONV_CH), 0)
    acc = u * cw[CONV_W - 1:CONV_W, :]
    for s in range(1, CONV_W):
        rolled = pltpu.roll(u, s, 0)
        head = jnp.where(row8 < s, pltpu.roll(prev8, s, 0), rolled[:SUBLANES])
        shifted = jnp.concatenate([head, rolled[SUBLANES:]], axis=0)
        acc = acc + shifted * cw[CONV_W - 1 - s:CONV_W - s, :]
    carry_ref[...] = u[tm - SUBLANES:, :]
    tail_ref[0] = u[tm - SUBLANES:, :]

    qa, ka, va, qb, kb, vb, beta, g, lg = _proj_tail(p, _silu(acc), hp_ref[...], w2_ref[...], gb_ref[...])
    if n_null:
        live = lax.broadcasted_iota(jnp.int32, (tm, 1), 0) >= n_null
        g = jnp.where(live, g, 0.0)
        lg = jnp.where(live, lg, 0.0)
    ltri = ltri_ref[...]
    gam = _tri_cumsum(ltri, g)
    cb = _tri_cumsum(ltri, lg)
    lane = lax.broadcasted_iota(jnp.int32, (tm, SMALL_W), 1)
    aux = jnp.where(lane < _G0, beta, jnp.where(lane < _GAM0, g, jnp.where(lane < _GLR0, gam, 0.0)))

    qa_ref[0] = qa
    ka_ref[0] = ka
    va_ref[0] = va
    q2_ref[0] = _dup_heads(qb)
    k2_ref[0] = _dup_heads(kb)
    vb_ref[0] = vb
    cb2_ref[0] = _dup_heads(cb)
    aux_ref[0] = aux


def _proj_prompt(x, carry0, P, *, tm, n_null):
    nb, length, _ = x.shape
    grid = (nb, length // tm)
    blk = np.arange(tm) // CHUNK
    ltri = jnp.asarray((blk[:, None] == blk[None, :]) & (np.arange(tm)[:, None] >= np.arange(tm)[None, :]), BF16)
    full = lambda shape: pl.BlockSpec(shape, lambda b, j: (0,) * len(shape))
    row = lambda w: pl.BlockSpec((1, tm, w), lambda b, j: (b, j, 0))
    out_w = (A_QK, A_QK, A_V, 2 * B_QK, 2 * B_QK, B_V, 2 * B_QK, SMALL_W)
    outs = pl.pallas_call(
        functools.partial(_proj_prompt_kernel, tm=tm, n_null=n_null),
        grid=grid,
        in_specs=[row(D_MODEL), full((SUBLANES, CONV_CH)), full((D_MODEL, SCAN_W)), full((CONV_W, CONV_CH)),
                  full((SUBLANES, SMALL_W)), full((SMALL_W, B_QK)), full((1, B_QK)), full((tm, tm))],
        out_specs=[row(w) for w in out_w] + [pl.BlockSpec((1, SUBLANES, CONV_CH), lambda b, j: (b, 0, 0))],
        out_shape=[jax.ShapeDtypeStruct((nb, length, w), F32) for w in out_w]
        + [jax.ShapeDtypeStruct((nb, SUBLANES, CONV_CH), F32)],
        scratch_shapes=[pltpu.VMEM((SUBLANES, CONV_CH), F32)],
        compiler_params=pltpu.CompilerParams(dimension_semantics=("arbitrary", "arbitrary"),
                                             vmem_limit_bytes=VMEM_LIMIT),
        name="proj_prompt",
    )(x, carry0, P["w_scan"], P["conv_w"], P["hp"], P["w2"], P["gb"], ltri)
    return outs


def _proj_decode_kernel(x_ref, st0_ref, st1_ref, st2_ref, w_ref, cw_ref, hp_ref, w2_ref, gb_ref,
                        qa_ref, ka_ref, va_ref, qb_ref, kb_ref, vb_ref, lg_ref, aux_ref, u_ref):
    p = _dot(_bf(x_ref[...]), w_ref[...])
    u = p[:, :CONV_CH]
    cw = cw_ref[...]
    acc = st0_ref[...] * cw[0:1, :] + st1_ref[...] * cw[1:2, :] + st2_ref[...] * cw[2:3, :] + u * cw[3:4, :]
    qa, ka, va, qb, kb, vb, beta, g, lg = _proj_tail(p, _silu(acc), hp_ref[...], w2_ref[...], gb_ref[...])
    lane = lax.broadcasted_iota(jnp.int32, beta.shape, 1)
    qa_ref[...] = qa
    ka_ref[...] = ka
    va_ref[...] = va
    qb_ref[...] = qb
    kb_ref[...] = kb
    vb_ref[...] = vb
    lg_ref[...] = lg
    aux_ref[...] = jnp.where(lane < _G0, beta, jnp.where(lane < _GAM0, g, 0.0))
    u_ref[...] = u


def _proj_decode(x, st0, st1, st2, P):
    n = x.shape[0]
    out_w = (A_QK, A_QK, A_V, B_QK, B_QK, B_V, B_QK, SMALL_W, CONV_CH)
    return pl.pallas_call(
        _proj_decode_kernel,
        out_shape=[jax.ShapeDtypeStruct((n, w), F32) for w in out_w],
        compiler_params=pltpu.CompilerParams(vmem_limit_bytes=VMEM_LIMIT),
        name="proj_decode",
    )(x, st0, st1, st2, P["w_scan"], P["conv_w"], P["hp"], P["w2"], P["gb"])


def _unit_lower_inverse(a):
    n = a.shape[0]
    eye = (lax.broadcasted_iota(jnp.int32, (n, n), 0) == lax.broadcasted_iota(jnp.int32, (n, n), 1)).astype(F32)
    x = eye - a
    y = _mm3(a, a)
    steps = int(np.log2(n)) - 1
    for i in range(steps):
        x = x + _mm3(x, y)
        if i + 1 < steps:
            y = _mm3(y, y)
    return x


def _gdn_kernel(q_ref, k_ref, v_ref, aux_ref, auxt_ref, s0_ref, o_ref, sfin_ref, s_ref):
    c = pl.program_id(1)

    @pl.when(c == 0)
    def _():
        s_ref[...] = s0_ref[0]

    ri = lax.broadcasted_iota(jnp.int32, (CHUNK, CHUNK), 0)
    ci = lax.broadcasted_iota(jnp.int32, (CHUNK, CHUNK), 1)
    for h in range(HEADS):
        sl = slice(h * A_DK, (h + 1) * A_DK)
        q = q_ref[0, :, sl]
        k = k_ref[0, :, sl]
        v = v_ref[0, :, sl]
        beta = aux_ref[0, :, _BETA0 + h:_BETA0 + h + 1]
        gc = aux_ref[0, :, _GAM0 + h:_GAM0 + h + 1]
        gr = auxt_ref[0, 0, _GAM0 + h:_GAM0 + h + 1, :]
        glast = auxt_ref[0, 0, _GAM0 + h:_GAM0 + h + 1, CHUNK - 1:CHUNK]
        dec = jnp.exp(jnp.where(ri >= ci, gc - gr, -jnp.inf))
        kb16 = _bf(k)
        kk = _dot_nt(kb16, kb16)
        a = jnp.where(ri > ci, beta * dec * kk, 0.0)
        t = _unit_lower_inverse(a)
        eg = jnp.exp(gc)
        sol = _mm3(t, jnp.concatenate([beta * v, (beta * eg) * k], axis=-1))
        u0 = sol[:, :A_DV]
        w = sol[:, A_DV:]
        qk = _dot_nt(_bf(q), kb16) * dec
        qd = q * eg
        kd = k * jnp.exp(glast - gc)
        s = s_ref[h]
        sb = _bf(s)
        u = u0 - _dot(_bf(w), sb)
        ub = _bf(u)
        o_ref[0, :, sl] = _dot(_bf(qd), sb) + _dot(_bf(qk), ub)
        s_ref[h] = jnp.exp(glast) * s + _dot_tn(_bf(kd), ub)

    @pl.when(c == pl.num_programs(1) - 1)
    def _():
        sfin_ref[0] = s_ref[...]


def _gdn_scan(qa, ka, va, aux, s0):
    nb, length, _ = qa.shape
    nc = length // CHUNK
    auxt = aux[:, :, :2 * SUBLANES].reshape(nb, nc, CHUNK, 2 * SUBLANES).transpose(0, 1, 3, 2)
    shared = s0.shape[0] == 1
    row = lambda w: pl.BlockSpec((1, CHUNK, w), lambda b, c: (b, c, 0))
    st = (1, HEADS, A_DK, A_DV)
    return pl.pallas_call(
        _gdn_kernel,
        grid=(nb, nc),
        in_specs=[row(A_QK), row(A_QK), row(A_V), row(SMALL_W),
                  pl.BlockSpec((1, 1, 2 * SUBLANES, CHUNK), lambda b, c: (b, c, 0, 0)),
                  pl.BlockSpec(st, (lambda b, c: (0, 0, 0, 0)) if shared else (lambda b, c: (b, 0, 0, 0)))],
        out_specs=[row(A_V), pl.BlockSpec(st, lambda b, c: (b, 0, 0, 0))],
        out_shape=[jax.ShapeDtypeStruct((nb, length, A_V), F32),
                   jax.ShapeDtypeStruct((nb, HEADS, A_DK, A_DV), F32)],
        scratch_shapes=[pltpu.VMEM((HEADS, A_DK, A_DV), F32)],
        compiler_params=pltpu.CompilerParams(dimension_semantics=("arbitrary", "arbitrary")),
        name="gdn_scan",
    )(qa, ka, va, aux, auxt, s0)


def _gla_kernel(q2_ref, k2_ref, c2_ref, v_ref, seg_ref, s0_ref, o_ref, sfin_ref, s_ref):
    c = pl.program_id(1)

    @pl.when(c == 0)
    def _():
        s_ref[...] = s0_ref[0]

    ri = lax.broadcasted_iota(jnp.int32, (CHUNK, CHUNK), 0)
    ci = lax.broadcasted_iota(jnp.int32, (CHUNK, CHUNK), 1)
    rblk = ri // SUB
    cblk = ci // SUB
    r16 = lax.broadcasted_iota(jnp.int32, (SUB, LANES), 0)
    l16 = lax.broadcasted_iota(jnp.int32, (SUB, LANES), 1)
    upper = (l16 >= B_DK).astype(jnp.int32)
    low1 = lax.broadcasted_iota(jnp.int32, (1, LANES), 1) < B_DK
    nsub = CHUNK // SUB
    for h in range(HEADS):
        lanes2 = slice(h * LANES, (h + 1) * LANES)
        lanes1 = slice(h * LANES, h * LANES + B_DK)
        q = q2_ref[0, :, lanes1]
        k = k2_ref[0, :, lanes1]
        cb = c2_ref[0, :, lanes1]
        v = v_ref[0, :, h * B_DV:(h + 1) * B_DV]

        zrows = []
        for i in range(nsub):
            rows = slice(i * SUB, (i + 1) * SUB)
            qq = q2_ref[0, rows, lanes2]
            cc = c2_ref[0, rows, lanes2]
            pieces = []
            for m in range(SUB // 2):
                r0 = i * SUB + 2 * m
                kp = jnp.where(low1, k2_ref[0, r0:r0 + 1, lanes2], k2_ref[0, r0 + 1:r0 + 2, lanes2])
                cp = jnp.where(low1, c2_ref[0, r0:r0 + 1, lanes2], c2_ref[0, r0 + 1:r0 + 2, lanes2])
                z = qq * kp * jnp.exp(jnp.minimum(cc - cp, 0.0))
                pieces.append(jnp.where(r16 >= 2 * m + upper, z, 0.0))
            zrows.append(jnp.concatenate(pieces, axis=-1))
        diag = _dot(_bf(jnp.concatenate(zrows, axis=0)), seg_ref[...])

        offs = [jnp.zeros((SUB, CHUNK), F32)]
        for i in range(1, nsub):
            rows = slice(i * SUB, (i + 1) * SUB)
            ref_row = c2_ref[0, i * SUB:i * SUB + 1, lanes1]
            qt = q[rows] * jnp.exp(cb[rows] - ref_row)
            kt = k * jnp.exp(jnp.minimum(ref_row - cb, 0.0))
            offs.append(_dot_nt(_bf(qt), _bf(kt)))
        off = jnp.concatenate(offs, axis=0)
        attn = jnp.where(cblk < rblk, off, jnp.where(cblk == rblk, diag, 0.0))

        s = s_ref[h]
        vb16 = _bf(v)
        o_ref[0, :, h * B_DV:(h + 1) * B_DV] = _dot(_bf(q * jnp.exp(cb)), _bf(s)) + _dot(_bf(attn), vb16)
        kt_ = k.T
        cbt = cb.T
        clast = cbt[:, CHUNK - 1:CHUNK]
        s_ref[h] = jnp.exp(clast) * s + _dot(_bf(kt_ * jnp.exp(clast - cbt)), vb16)

    @pl.when(c == pl.num_programs(1) - 1)
    def _():
        sfin_ref[0] = s_ref[...]


def _gla_scan(q2, k2, cb2, vb, s0):
    nb, length, _ = q2.shape
    nc = length // CHUNK
    rows = np.arange(SUB // 2 * LANES)
    jj = 2 * (rows // LANES) + (rows % LANES) // B_DK
    seg = jnp.asarray(jj[:, None] == (np.arange(CHUNK)[None, :] % SUB), BF16)
    shared = s0.shape[0] == 1
    row = lambda w: pl.BlockSpec((1, CHUNK, w), lambda b, c: (b, c, 0))
    st = (1, HEADS, B_DK, B_DV)
    return pl.pallas_call(
        _gla_kernel,
        grid=(nb, nc),
        in_specs=[row(2 * B_QK), row(2 * B_QK), row(2 * B_QK), row(B_V),
                  pl.BlockSpec(seg.shape, lambda b, c: (0, 0)),
                  pl.BlockSpec(st, (lambda b, c: (0, 0, 0, 0)) if shared else (lambda b, c: (b, 0, 0, 0)))],
        out_specs=[row(B_V), pl.BlockSpec(st, lambda b, c: (b, 0, 0, 0))],
        out_shape=[jax.ShapeDtypeStruct((nb, length, B_V), F32),
                   jax.ShapeDtypeStruct((nb, HEADS, B_DK, B_DV), F32)],
        scratch_shapes=[pltpu.VMEM((HEADS, B_DK, B_DV), F32)],
        compiler_params=pltpu.CompilerParams(dimension_semantics=("arbitrary", "arbitrary")),
        name="gla_scan",
    )(q2, k2, cb2, vb, seg, s0)


def _decode_kernel(qa_ref, ka_ref, va_ref, aux_ref, qb_ref, kb_ref, vb_ref, lg_ref, sg_ref, sl_ref,
                   oa_ref, ob_ref, sgn_ref, sln_ref, *, bt):
    qat = qa_ref[...].T
    kat = ka_ref[...].T
    qbt = qb_ref[...].T
    kbt = kb_ref[...].T
    decay_t = jnp.exp(lg_ref[...]).T
    for b in range(bt):
        for h in range(HEADS):
            s = sg_ref[b, h]
            kcol = kat[h * A_DK:(h + 1) * A_DK, b:b + 1]
            qcol = qat[h * A_DK:(h + 1) * A_DK, b:b + 1]
            beta = aux_ref[b:b + 1, _BETA0 + h:_BETA0 + h + 1]
            eg = jnp.exp(aux_ref[b:b + 1, _G0 + h:_G0 + h + 1])
            r = jnp.sum(kcol * s, axis=0, keepdims=True)
            u = beta * (va_ref[b:b + 1, h * A_DV:(h + 1) * A_DV] - eg * r)
            sn = eg * s + kcol * u
            sgn_ref[b, h] = sn
            oa_ref[b:b + 1, h * A_DV:(h + 1) * A_DV] = jnp.sum(qcol * sn, axis=0, keepdims=True)

            s = sl_ref[b, h]
            kcol = kbt[h * B_DK:(h + 1) * B_DK, b:b + 1]
            qcol = qbt[h * B_DK:(h + 1) * B_DK, b:b + 1]
            dcol = decay_t[h * B_DK:(h + 1) * B_DK, b:b + 1]
            sn = dcol * s + kcol * vb_ref[b:b + 1, h * B_DV:(h + 1) * B_DV]
            sln_ref[b, h] = sn
            ob_ref[b:b + 1, h * B_DV:(h + 1) * B_DV] = jnp.sum(qcol * sn, axis=0, keepdims=True)


def _decode_mix(qa, ka, va, aux, qb, kb, vb, lg, sg, sl, *, bt=8):
    n = qa.shape[0]
    row = lambda w: pl.BlockSpec((bt, w), lambda i: (i, 0))
    sgs = pl.BlockSpec((bt, HEADS, A_DK, A_DV), lambda i: (i, 0, 0, 0))
    sls = pl.BlockSpec((bt, HEADS, B_DK, B_DV), lambda i: (i, 0, 0, 0))
    return pl.pallas_call(
        functools.partial(_decode_kernel, bt=bt),
        grid=(n // bt,),
        in_specs=[row(A_QK), row(A_QK), row(A_V), row(SMALL_W), row(B_QK), row(B_QK), row(B_V), row(B_QK), sgs, sls],
        out_specs=[row(A_V), row(B_V), sgs, sls],
        out_shape=[jax.ShapeDtypeStruct((n, A_V), F32), jax.ShapeDtypeStruct((n, B_V), F32),
                   jax.ShapeDtypeStruct(sg.shape, F32), jax.ShapeDtypeStruct(sl.shape, F32)],
        compiler_params=pltpu.CompilerParams(dimension_semantics=("arbitrary",)),
        name="decode_mix",
    )(qa, ka, va, aux, qb, kb, vb, lg, sg, sl)


def _head_rms(o, dv):
    parts = []
    for h in range(HEADS):
        oh = o[:, h * dv:(h + 1) * dv]
        parts.append(oh * lax.rsqrt(jnp.mean(oh * oh, axis=-1, keepdims=True) + 1e-6))
    return jnp.concatenate(parts, axis=-1)


def _mix_kernel(x_ref, oa_ref, ob_ref, wg_ref, na_ref, nb_ref, wa_ref, wb_ref, wo_ref, g_ref, b_ref, out_ref,
                *, alpha):
    x = x_ref[...]
    gates = _dot(_bf(x), wg_ref[...])
    za = gates[:, :A_V]
    rb = gates[:, A_V:A_V + B_V]
    ga = gates[:, A_V + B_V:A_V + B_V + D_MODEL]
    gb = gates[:, A_V + B_V + D_MODEL:]
    oa = _head_rms(oa_ref[...], A_DV) * na_ref[...] * _silu(za)
    ob = _head_rms(ob_ref[...], B_DV) * nb_ref[...] * _silu(rb)
    ya = _dot(_bf(oa), wa_ref[...])
    yb = _dot(_bf(ob), wb_ref[...])
    merged = _sigmoid(ga) * ya + _sigmoid(gb) * yb
    mix = _dot(_bf(merged), wo_ref[...])
    out_ref[...] = _layer_norm(alpha * x + mix, g_ref[...], b_ref[...])


def _mix(x, oa, ob, P, *, tm, alpha):
    n = x.shape[0]
    full = lambda a: pl.BlockSpec(a.shape, lambda i: (0,) * a.ndim)
    row = lambda w: pl.BlockSpec((tm, w), lambda i: (i, 0))
    ws = (P["w_gate"], P["na"], P["nb"], P["wa"], P["wb"], P["wo"], P["ln1g"], P["ln1b"])
    return pl.pallas_call(
        functools.partial(_mix_kernel, alpha=alpha),
        grid=(n // tm,),
        in_specs=[row(D_MODEL), row(A_V), row(B_V)] + [full(a) for a in ws],
        out_specs=row(D_MODEL),
        out_shape=jax.ShapeDtypeStruct((n, D_MODEL), F32),
        compiler_params=pltpu.CompilerParams(dimension_semantics=("arbitrary",), vmem_limit_bytes=VMEM_LIMIT),
        name="mix",
    )(x, oa, ob, *ws)


def _ffn_kernel(x_ref, wi_ref, wo_ref, g_ref, b_ref, out_ref, *, alpha, d_ff, fc):
    x = x_ref[...]
    xb = _bf(x)
    acc = jnp.zeros(x.shape, F32)
    for c in range(d_ff // fc):
        a = _dot(xb, wi_ref[:, c * fc:(c + 1) * fc])
        u = _dot(xb, wi_ref[:, d_ff + c * fc:d_ff + (c + 1) * fc])
        acc = acc + _dot(_bf(_silu(a) * u), wo_ref[c * fc:(c + 1) * fc, :])
    out_ref[...] = _layer_norm(alpha * x + acc, g_ref[...], b_ref[...])


def _ffn(x, P, *, tm, alpha):
    n = x.shape[0]
    d_ff = P["wfo"].shape[0]
    full = lambda a: pl.BlockSpec(a.shape, lambda i: (0,) * a.ndim)
    row = pl.BlockSpec((tm, D_MODEL), lambda i: (i, 0))
    ws = (P["wfi"], P["wfo"], P["ln2g"], P["ln2b"])
    return pl.pallas_call(
        functools.partial(_ffn_kernel, alpha=alpha, d_ff=d_ff, fc=256),
        grid=(n // tm,),
        in_specs=[row] + [full(a) for a in ws],
        out_specs=row,
        out_shape=jax.ShapeDtypeStruct((n, D_MODEL), F32),
        compiler_params=pltpu.CompilerParams(dimension_semantics=("arbitrary",), vmem_limit_bytes=VMEM_LIMIT),
        name="ffn",
    )(x, *ws)


def _layer_params(l, w_in, conv_w, a_log, dt_bias, gdn_norm_w, gla_gate_w2, gla_gate_b, gla_norm_w,
                  w_branch_a, w_branch_b, w_out, ln1_g, ln1_b, ln2_g, ln2_b, w_ffn_in, w_ffn_out):
    sizes = (CONV_CH, A_V, HEADS, HEADS, B_QK, B_QK, B_V, B_V, B_GATE_RANK, D_MODEL, D_MODEL)
    offs = np.concatenate([[0], np.cumsum(sizes)])
    wi = w_in[l]
    col = lambda i: wi[:, offs[i]:offs[i + 1]]
    w_qkv, w_z, w_beta, w_a, w_qb, w_kb, w_vb, w_rb, w_glr, w_ga, w_gb = [col(i) for i in range(len(sizes))]
    pad = jnp.zeros((D_MODEL, SMALL_W - _GLR0 - B_GATE_RANK), F32)
    w_small = jnp.concatenate([w_beta, w_a, w_a, w_glr, pad], axis=1)
    lane_vec = lambda v: jnp.zeros((SMALL_W,), F32).at[_G0:_G0 + HEADS].set(v).at[_GAM0:_GAM0 + HEADS].set(v)
    hp = jnp.zeros((SUBLANES, SMALL_W), F32).at[0].set(lane_vec(a_log[l])).at[1].set(lane_vec(dt_bias[l]))
    w2 = jnp.zeros((SMALL_W, B_QK), F32).at[_GLR0:_GLR0 + B_GATE_RANK].set(gla_gate_w2[l])
    return {
        "w_scan": _bf(jnp.concatenate([w_qkv, w_qb, w_kb, w_vb, w_small], axis=1)),
        "w_gate": _bf(jnp.concatenate([w_z, w_rb, w_ga, w_gb], axis=1)),
        "conv_w": conv_w[l],
        "hp": hp,
        "w2": _bf(w2),
        "gb": gla_gate_b[l][None, :],
        "na": jnp.tile(gdn_norm_w[l], HEADS)[None, :],
        "nb": jnp.tile(gla_norm_w[l], HEADS)[None, :],
        "wa": _bf(w_branch_a[l]),
        "wb": _bf(w_branch_b[l]),
        "wo": _bf(w_out[l]),
        "ln1g": ln1_g[l][None, :],
        "ln1b": ln1_b[l][None, :],
        "ln2g": ln2_g[l][None, :],
        "ln2b": ln2_b[l][None, :],
        "wfi": _bf(w_ffn_in[l]),
        "wfo": _bf(w_ffn_out[l]),
    }


def kernel(x_prompt, x_sample, state_gdn, state_gla, state_conv, meta_tokens, w_in, conv_w, a_log, dt_bias,
           gdn_norm_w, gla_gate_w2, gla_gate_b, gla_norm_w, w_branch_a, w_branch_b, w_out,
           ln1_g, ln1_b, ln2_g, ln2_b, w_ffn_in, w_ffn_out):
    depth = w_in.shape[0]
    alpha = (2.0 * depth) ** 0.25
    nb, seq, _ = x_prompt.shape
    assert seq % CHUNK == 0 and x_sample.shape[1] == 1 and meta_tokens.shape[0] == N_META
    tm_proj = 256
    tm_dense = 256
    null_rows = CHUNK - N_META

    xp = x_prompt
    xs = x_sample[:, 0, :]
    xm = meta_tokens
    zeros_rows = jnp.zeros((null_rows, D_MODEL), F32)
    outs = {k: [] for k in ("gdn_p", "gla_p", "conv_p", "gdn_s", "gla_s", "conv_s")}
    for l in range(depth):
        P = _layer_params(l, w_in, conv_w, a_log, dt_bias, gdn_norm_w, gla_gate_w2, gla_gate_b, gla_norm_w,
                          w_branch_a, w_branch_b, w_out, ln1_g, ln1_b, ln2_g, ln2_b, w_ffn_in, w_ffn_out)

        xm_pad = jnp.concatenate([zeros_rows, xm], axis=0)[None]
        qa, ka, va, q2, k2, vb, cb2, aux, tail_m = _proj_prompt(
            xm_pad, jnp.zeros((SUBLANES, CONV_CH), F32), P, tm=CHUNK, n_null=null_rows)
        oa_m, sg_m = _gdn_scan(qa, ka, va, aux, jnp.zeros((1, HEADS, A_DK, A_DV), F32))
        ob_m, sl_m = _gla_scan(q2, k2, cb2, vb, jnp.zeros((1, HEADS, B_DK, B_DV), F32))

        qa, ka, va, q2, k2, vb, cb2, aux, tail_p = _proj_prompt(xp, tail_m[0], P, tm=tm_proj, n_null=0)
        oa_p, sg_p = _gdn_scan(qa, ka, va, aux, sg_m)
        ob_p, sl_p = _gla_scan(q2, k2, cb2, vb, sl_m)
        x1 = _mix(xp.reshape(nb * seq, D_MODEL), oa_p.reshape(nb * seq, A_V), ob_p.reshape(nb * seq, B_V), P,
                  tm=tm_dense, alpha=alpha)
        xp = _ffn(x1, P, tm=tm_dense, alpha=alpha).reshape(nb, seq, D_MODEL)
        outs["gdn_p"].append(sg_p)
        outs["gla_p"].append(sl_p)
        outs["conv_p"].append(tail_p[:, SUBLANES - (CONV_W - 1):, :])

        if l + 1 < depth:
            x1m = _mix(xm, oa_m[0, null_rows:], ob_m[0, null_rows:], P, tm=N_META, alpha=alpha)
            xm = _ffn(x1m, P, tm=N_META, alpha=alpha)

        st = state_conv[l]
        qa, ka, va, qb, kb, vb, lg, aux, u = _proj_decode(xs, st[:, 0], st[:, 1], st[:, 2], P)
        oa_s, ob_s, sg_s, sl_s = _decode_mix(qa, ka, va, aux, qb, kb, vb, lg, state_gdn[l], state_gla[l])
        ns = xs.shape[0]
        xs = _ffn(_mix(xs, oa_s, ob_s, P, tm=ns, alpha=alpha), P, tm=ns, alpha=alpha)
        outs["gdn_s"].append(sg_s)
        outs["gla_s"].append(sl_s)
        outs["conv_s"].append(jnp.stack([st[:, 1], st[:, 2], u], axis=1))

    return (xp, xs[:, None, :], jnp.stack(outs["gdn_p"]), jnp.stack(outs["gla_p"]), jnp.stack(outs["conv_p"]),
            jnp.stack(outs["gdn_s"]), jnp.stack(outs["gla_s"]), jnp.stack(outs["conv_s"]))
```

```python
import functools
import itertools

import numpy as np
import jax
import jax.numpy as jnp
from jax import lax
from jax.experimental import pallas as pl
from jax.experimental.pallas import tpu as pltpu

F32 = jnp.float32
BF16 = jnp.bfloat16

D_MODEL = 1024
N_META = 16
CHUNK = 64
HEADS = 4
A_DK = 128
A_DV = 128
A_QK = HEADS * A_DK
A_V = HEADS * A_DV
CONV_W = 4
CONV_CH = 2 * A_QK + A_V
B_DK = 64
B_DV = 128
B_QK = HEADS * B_DK
B_V = HEADS * B_DV
B_GATE_RANK = 16
B_GATE_TAU = 16.0
SUB = 16
LANES = 128
SUBLANES = 8
SMALL_W = LANES
SCAN_W = CONV_CH + 2 * B_QK + B_V + SMALL_W
GATE_W = A_V + B_V + 2 * D_MODEL
VMEM_LIMIT = 56 * 1024 * 1024

_BETA0, _G0, _GAM0, _GLR0 = 0, 4, 8, 12


def _dot(a, b):
    return jnp.dot(a, b, preferred_element_type=F32)


def _dot_nt(a, b):
    return lax.dot_general(a, b, (((1,), (1,)), ((), ())), preferred_element_type=F32)


def _dot_tn(a, b):
    return lax.dot_general(a, b, (((0,), (0,)), ((), ())), preferred_element_type=F32)


def _bf(x):
    return x.astype(BF16)


def _mm1(a, b):
    return _dot(_bf(a), _bf(b))


def _tri_cumsum(ltri, x):
    hi = x.astype(BF16)
    r1 = x - hi.astype(F32)
    mid = r1.astype(BF16)
    lo = (r1 - mid.astype(F32)).astype(BF16)
    return _dot(ltri, hi) + _dot(ltri, mid) + _dot(ltri, lo)


def _sigmoid(x):
    return 1.0 / (1.0 + jnp.exp(-x))


def _silu(x):
    return x * _sigmoid(x)


def _softplus(x):
    return jnp.maximum(x, 0.0) + jnp.log1p(jnp.exp(-jnp.abs(x)))


def _log_sigmoid(x):
    return jnp.minimum(x, 0.0) - jnp.log1p(jnp.exp(-jnp.abs(x)))


def _layer_norm(h, g, b):
    mu = jnp.mean(h, axis=-1, keepdims=True)
    d = h - mu
    var = jnp.mean(d * d, axis=-1, keepdims=True)
    return d * lax.rsqrt(var + 1e-5) * g + b


def _proj_tail(p, act, hp, w2, gb):
    qs, ks = [], []
    for h in range(HEADS):
        qh = act[:, h * A_DK:(h + 1) * A_DK]
        qs.append(qh * lax.rsqrt(jnp.sum(qh * qh, axis=-1, keepdims=True) + 1e-6) * (A_DK ** -0.5))
        kh = act[:, A_QK + h * A_DK:A_QK + (h + 1) * A_DK]
        ks.append(kh * lax.rsqrt(jnp.sum(kh * kh, axis=-1, keepdims=True) + 1e-6))
    qa = jnp.concatenate(qs, axis=-1)
    ka = jnp.concatenate(ks, axis=-1)
    va = act[:, 2 * A_QK:]
    qb = p[:, CONV_CH:CONV_CH + B_QK] * (B_DK ** -0.5)
    kb = p[:, CONV_CH + B_QK:CONV_CH + 2 * B_QK]
    vb = p[:, CONV_CH + 2 * B_QK:CONV_CH + 2 * B_QK + B_V]
    small = p[:, CONV_CH + 2 * B_QK + B_V:]
    beta = _sigmoid(small)
    g = -(jnp.exp(hp[0:1, :]) * _softplus(small + hp[1:2, :]))
    lg = _log_sigmoid(_dot(_bf(small), w2) + gb) * (1.0 / B_GATE_TAU)
    return qa, ka, va, qb, kb, vb, beta, g, lg


def _dup_heads(x):
    parts = []
    for h in range(HEADS):
        xh = x[:, h * B_DK:(h + 1) * B_DK]
        parts += [xh, xh]
    return jnp.concatenate(parts, axis=-1)


def _proj_prompt_kernel(x_ref, carry0_ref, w_ref, cw_ref, hp_ref, w2_ref, gb_ref, ltri_ref,
                        qa_ref, ka_ref, va_ref, q2_ref, k2_ref, vb_ref, cb2_ref, aux_ref, tail_ref,
                        carry_ref, *, tm, n_null):
    j = pl.program_id(1)

    @pl.when(j == 0)
    def _():
        carry_ref[...] = carry0_ref[...]

    p = _dot(_bf(x_ref[0]), w_ref[...])
    u = p[:, :CONV_CH]
    prev8 = carry_ref[...]
    cw = cw_ref[...]
    row8 = lax.broadcasted_iota(jnp.int32, (SUBLANES, CONV_CH), 0)
    acc = u * cw[CONV_W - 1:CONV_W, :]
    for s in range(1, CONV_W):
        rolled = pltpu.roll(u, s, 0)
        head = jnp.where(row8 < s, pltpu.roll(prev8, s, 0), rolled[:SUBLANES])
        shifted = jnp.concatenate([head, rolled[SUBLANES:]], axis=0)
        acc = acc + shifted * cw[CONV_W - 1 - s:CONV_W - s, :]
    carry_ref[...] = u[tm - SUBLANES:, :]
    tail_ref[0] = u[tm - SUBLANES:, :]

    qa, ka, va, qb, kb, vb, beta, g, lg = _proj_tail(p, _silu(acc), hp_ref[...], w2_ref[...], gb_ref[...])
    if n_null:
        live = lax.broadcasted_iota(jnp.int32, (tm, 1), 0) >= n_null
        g = jnp.where(live, g, 0.0)
        lg = jnp.where(live, lg, 0.0)
    ltri = ltri_ref[...]
    gam = _tri_cumsum(ltri, g)
    cb = _tri_cumsum(ltri, lg)
    lane = lax.broadcasted_iota(jnp.int32, (tm, SMALL_W), 1)
    aux = jnp.where(lane < _G0, beta, jnp.where(lane < _GAM0, g, jnp.where(lane < _GLR0, gam, 0.0)))

    qa_ref[0] = qa
    ka_ref[0] = ka
    va_ref[0] = va
    q2_ref[0] = _dup_heads(qb)
    k2_ref[0] = _dup_heads(kb)
    vb_ref[0] = vb
    cb2_ref[0] = _dup_heads(cb)
    aux_ref[0] = aux


def _proj_prompt(x, carry0, P, *, tm, n_null):
    nb, length, _ = x.shape
    grid = (nb, length // tm)
    blk = np.arange(tm) // CHUNK
    ltri = jnp.asarray((blk[:, None] == blk[None, :]) & (np.arange(tm)[:, None] >= np.arange(tm)[None, :]), BF16)
    full = lambda shape: pl.BlockSpec(shape, lambda b, j: (0,) * len(shape))
    row = lambda w: pl.BlockSpec((1, tm, w), lambda b, j: (b, j, 0))
    out_w = (A_QK, A_QK, A_V, 2 * B_QK, 2 * B_QK, B_V, 2 * B_QK, SMALL_W)
    outs = pl.pallas_call(
        functools.partial(_proj_prompt_kernel, tm=tm, n_null=n_null),
        grid=grid,
        in_specs=[row(D_MODEL), full((SUBLANES, CONV_CH)), full((D_MODEL, SCAN_W)), full((CONV_W, CONV_CH)),
                  full((SUBLANES, SMALL_W)), full((SMALL_W, B_QK)), full((1, B_QK)), full((tm, tm))],
        out_specs=[row(w) for w in out_w] + [pl.BlockSpec((1, SUBLANES, CONV_CH), lambda b, j: (b, 0, 0))],
        out_shape=[jax.ShapeDtypeStruct((nb, length, w), F32) for w in out_w]
        + [jax.ShapeDtypeStruct((nb, SUBLANES, CONV_CH), F32)],
        scratch_shapes=[pltpu.VMEM((SUBLANES, CONV_CH), F32)],
        compiler_params=pltpu.CompilerParams(dimension_semantics=("arbitrary", "arbitrary"),
                                             vmem_limit_bytes=VMEM_LIMIT),
        name="proj_prompt",
    )(x, carry0, P["w_scan"], P["conv_w"], P["hp"], P["w2"], P["gb"], ltri)
    return outs


def _proj_decode_kernel(x_ref, st0_ref, st1_ref, st2_ref, w_ref, cw_ref, hp_ref, w2_ref, gb_ref,
                        qa_ref, ka_ref, va_ref, qb_ref, kb_ref, vb_ref, lg_ref, aux_ref, u_ref):
    p = _dot(_bf(x_ref[...]), w_ref[...])
    u = p[:, :CONV_CH]
    cw = cw_ref[...]
    acc = st0_ref[...] * cw[0:1, :] + st1_ref[...] * cw[1:2, :] + st2_ref[...] * cw[2:3, :] + u * cw[3:4, :]
    qa, ka, va, qb, kb, vb, beta, g, lg = _proj_tail(p, _silu(acc), hp_ref[...], w2_ref[...], gb_ref[...])
    lane = lax.broadcasted_iota(jnp.int32, beta.shape, 1)
    qa_ref[...] = qa
    ka_ref[...] = ka
    va_ref[...] = va
    qb_ref[...] = qb
    kb_ref[...] = kb
    vb_ref[...] = vb
    lg_ref[...] = lg
    aux_ref[...] = jnp.where(lane < _G0, beta, jnp.where(lane < _GAM0, g, 0.0))
    u_ref[...] = u


def _proj_decode(x, st0, st1, st2, P):
    n = x.shape[0]
    out_w = (A_QK, A_QK, A_V, B_QK, B_QK, B_V, B_QK, SMALL_W, CONV_CH)
    return pl.pallas_call(
        _proj_decode_kernel,
        out_shape=[jax.ShapeDtypeStruct((n, w), F32) for w in out_w],
        compiler_params=pltpu.CompilerParams(vmem_limit_bytes=VMEM_LIMIT),
        name="proj_decode",
    )(x, st0, st1, st2, P["w_scan"], P["conv_w"], P["hp"], P["w2"], P["gb"])


def _unit_lower_inverse(a):
    n = a.shape[0]
    eye = (lax.broadcasted_iota(jnp.int32, (n, n), 0) == lax.broadcasted_iota(jnp.int32, (n, n), 1)).astype(F32)
    x = eye - a
    y = _mm1(a, a)
    steps = int(np.log2(n)) - 1
    for i in range(steps):
        x = x + _mm1(x, y)
        if i + 1 < steps:
            y = _mm1(y, y)
    return x


def _gdn_kernel(q_ref, k_ref, v_ref, aux_ref, auxt_ref, s0_ref, o_ref, sfin_ref, s_ref, *, bt):
    c = pl.program_id(1)

    @pl.when(c == 0)
    def _():
        for b in range(bt):
            s_ref[b] = s0_ref[0]

    ri = lax.broadcasted_iota(jnp.int32, (CHUNK, CHUNK), 0)
    ci = lax.broadcasted_iota(jnp.int32, (CHUNK, CHUNK), 1)
    eye = (ri == ci).astype(F32)
    chains = list(itertools.product(range(bt), range(HEADS)))
    lanes = lambda h: slice(h * A_DK, (h + 1) * A_DK)
    col = lambda b, h, off: aux_ref[b, :, off + h:off + h + 1]
    kb = [_bf(k_ref[b, :, lanes(h)]) for b, h in chains]
    kk = [_dot_nt(x, x) for x in kb]
    qk = [_dot_nt(_bf(q_ref[b, :, lanes(h)]), kb[g]) for g, (b, h) in enumerate(chains)]
    dec = [jnp.exp(jnp.where(ri >= ci, col(b, h, _GAM0) - auxt_ref[b, 0, _GAM0 + h:_GAM0 + h + 1, :], -jnp.inf))
           for b, h in chains]
    a = [jnp.where(ri > ci, col(b, h, _BETA0) * dec[g] * kk[g], 0.0) for g, (b, h) in enumerate(chains)]
    ab = [_bf(x) for x in a]
    x = [eye - m for m in a]
    y = [_dot(m, m) for m in ab]
    steps = int(np.log2(CHUNK)) - 1
    for i in range(steps):
        yb = [_bf(m) for m in y]
        x = [m + _dot(_bf(m), n) for m, n in zip(x, yb)]
        if i + 1 < steps:
            y = [_dot(n, n) for n in yb]
    eg = [jnp.exp(col(b, h, _GAM0)) for b, h in chains]
    rhs = [_bf(jnp.concatenate([col(b, h, _BETA0) * v_ref[b, :, lanes(h)],
                                (col(b, h, _BETA0) * eg[g]) * k_ref[b, :, lanes(h)]], axis=-1))
           for g, (b, h) in enumerate(chains)]
    sol = [_dot(_bf(m), r) for m, r in zip(x, rhs)]
    sb = [_bf(s_ref[b, h]) for b, h in chains]
    u = [_bf(sol[g][:, :A_DV] - _dot(_bf(sol[g][:, A_DV:]), sb[g])) for g in range(len(chains))]
    for g, (b, h) in enumerate(chains):
        qd = _bf(q_ref[b, :, lanes(h)] * eg[g])
        o_ref[b, :, lanes(h)] = _dot(qd, sb[g]) + _dot(_bf(qk[g] * dec[g]), u[g])
    for g, (b, h) in enumerate(chains):
        glast = auxt_ref[b, 0, _GAM0 + h:_GAM0 + h + 1, CHUNK - 1:CHUNK]
        kd = _bf(k_ref[b, :, lanes(h)] * jnp.exp(glast - col(b, h, _GAM0)))
        s_ref[b, h] = jnp.exp(glast) * s_ref[b, h] + _dot_tn(kd, u[g])

    @pl.when(c == pl.num_programs(1) - 1)
    def _():
        sfin_ref[...] = s_ref[...]


def _gdn_scan(qa, ka, va, aux, s0, *, bt):
    nb, length, _ = qa.shape
    nc = length // CHUNK
    auxt = aux[:, :, :2 * SUBLANES].reshape(nb, nc, CHUNK, 2 * SUBLANES).transpose(0, 1, 3, 2)
    row = lambda w: pl.BlockSpec((bt, CHUNK, w), lambda b, c: (b, c, 0))
    return pl.pallas_call(
        functools.partial(_gdn_kernel, bt=bt),
        grid=(nb // bt, nc),
        in_specs=[row(A_QK), row(A_QK), row(A_V), row(SMALL_W),
                  pl.BlockSpec((bt, 1, 2 * SUBLANES, CHUNK), lambda b, c: (b, c, 0, 0)),
                  pl.BlockSpec((1, HEADS, A_DK, A_DV), lambda b, c: (0, 0, 0, 0))],
        out_specs=[row(A_V), pl.BlockSpec((bt, HEADS, A_DK, A_DV), lambda b, c: (b, 0, 0, 0))],
        out_shape=[jax.ShapeDtypeStruct((nb, length, A_V), F32),
                   jax.ShapeDtypeStruct((nb, HEADS, A_DK, A_DV), F32)],
        scratch_shapes=[pltpu.VMEM((bt, HEADS, A_DK, A_DV), F32)],
        compiler_params=pltpu.CompilerParams(dimension_semantics=("arbitrary", "arbitrary")),
        name="gdn_scan",
    )(qa, ka, va, aux, auxt, s0)


def _gla_kernel(q2_ref, k2_ref, c2_ref, v_ref, seg_ref, s0_ref, o_ref, sfin_ref, s_ref, *, bt):
    c = pl.program_id(1)

    @pl.when(c == 0)
    def _():
        for b in range(bt):
            s_ref[b] = s0_ref[0]

    ri = lax.broadcasted_iota(jnp.int32, (CHUNK, CHUNK), 0)
    ci = lax.broadcasted_iota(jnp.int32, (CHUNK, CHUNK), 1)
    rblk = ri // SUB
    cblk = ci // SUB
    r16 = lax.broadcasted_iota(jnp.int32, (SUB, LANES), 0)
    l16 = lax.broadcasted_iota(jnp.int32, (SUB, LANES), 1)
    upper = (l16 >= B_DK).astype(jnp.int32)
    low1 = lax.broadcasted_iota(jnp.int32, (1, LANES), 1) < B_DK
    nsub = CHUNK // SUB
    for b, h in itertools.product(range(bt), range(HEADS)):
        lanes2 = slice(h * LANES, (h + 1) * LANES)
        lanes1 = slice(h * LANES, h * LANES + B_DK)
        q = q2_ref[b,:, lanes1]
        k = k2_ref[b,:, lanes1]
        cb = c2_ref[b,:, lanes1]
        v = v_ref[b, :, h * B_DV:(h + 1) * B_DV]

        zrows = []
        for i in range(nsub):
            rows = slice(i * SUB, (i + 1) * SUB)
            qq = q2_ref[b,rows, lanes2]
            cc = c2_ref[b,rows, lanes2]
            pieces = []
            for m in range(SUB // 2):
                r0 = i * SUB + 2 * m
                kp = jnp.where(low1, k2_ref[b,r0:r0 + 1, lanes2], k2_ref[b,r0 + 1:r0 + 2, lanes2])
                cp = jnp.where(low1, c2_ref[b,r0:r0 + 1, lanes2], c2_ref[b,r0 + 1:r0 + 2, lanes2])
                z = qq * kp * jnp.exp(jnp.minimum(cc - cp, 0.0))
                pieces.append(jnp.where(r16 >= 2 * m + upper, z, 0.0))
            zrows.append(jnp.concatenate(pieces, axis=-1))
        diag = _dot(_bf(jnp.concatenate(zrows, axis=0)), seg_ref[...])

        offs = [jnp.zeros((SUB, CHUNK), F32)]
        for i in range(1, nsub):
            rows = slice(i * SUB, (i + 1) * SUB)
            ref_row = c2_ref[b,i * SUB:i * SUB + 1, lanes1]
            qt = q[rows] * jnp.exp(cb[rows] - ref_row)
            kt = k * jnp.exp(jnp.minimum(ref_row - cb, 0.0))
            offs.append(_dot_nt(_bf(qt), _bf(kt)))
        off = jnp.concatenate(offs, axis=0)
        attn = jnp.where(cblk < rblk, off, jnp.where(cblk == rblk, diag, 0.0))

        s = s_ref[b, h]
        vb16 = _bf(v)
        o_ref[b, :, h * B_DV:(h + 1) * B_DV] = _dot(_bf(q * jnp.exp(cb)), _bf(s)) + _dot(_bf(attn), vb16)
        kt_ = k.T
        cbt = cb.T
        clast = cbt[:, CHUNK - 1:CHUNK]
        s_ref[b, h] = jnp.exp(clast) * s + _dot(_bf(kt_ * jnp.exp(clast - cbt)), vb16)

    @pl.when(c == pl.num_programs(1) - 1)
    def _():
        sfin_ref[...] = s_ref[...]


def _gla_scan(q2, k2, cb2, vb, s0, *, bt):
    nb, length, _ = q2.shape
    nc = length // CHUNK
    rows = np.arange(SUB // 2 * LANES)
    jj = 2 * (rows // LANES) + (rows % LANES) // B_DK
    seg = jnp.asarray(jj[:, None] == (np.arange(CHUNK)[None, :] % SUB), BF16)
    row = lambda w: pl.BlockSpec((bt, CHUNK, w), lambda b, c: (b, c, 0))
    return pl.pallas_call(
        functools.partial(_gla_kernel, bt=bt),
        grid=(nb // bt, nc),
        in_specs=[row(2 * B_QK), row(2 * B_QK), row(2 * B_QK), row(B_V),
                  pl.BlockSpec(seg.shape, lambda b, c: (0, 0)),
                  pl.BlockSpec((1, HEADS, B_DK, B_DV), lambda b, c: (0, 0, 0, 0))],
        out_specs=[row(B_V), pl.BlockSpec((bt, HEADS, B_DK, B_DV), lambda b, c: (b, 0, 0, 0))],
        out_shape=[jax.ShapeDtypeStruct((nb, length, B_V), F32),
                   jax.ShapeDtypeStruct((nb, HEADS, B_DK, B_DV), F32)],
        scratch_shapes=[pltpu.VMEM((bt, HEADS, B_DK, B_DV), F32)],
        compiler_params=pltpu.CompilerParams(dimension_semantics=("arbitrary", "arbitrary")),
        name="gla_scan",
    )(q2, k2, cb2, vb, seg, s0)


def _decode_kernel(qa_ref, ka_ref, va_ref, aux_ref, qb_ref, kb_ref, vb_ref, lg_ref, sg_ref, sl_ref,
                   oa_ref, ob_ref, sgn_ref, sln_ref, *, bt):
    qat = qa_ref[...].T
    kat = ka_ref[...].T
    qbt = qb_ref[...].T
    kbt = kb_ref[...].T
    decay_t = jnp.exp(lg_ref[...]).T
    for b in range(bt):
        for h in range(HEADS):
            s = sg_ref[b, h]
            kcol = kat[h * A_DK:(h + 1) * A_DK, b:b + 1]
            qcol = qat[h * A_DK:(h + 1) * A_DK, b:b + 1]
            beta = aux_ref[b:b + 1, _BETA0 + h:_BETA0 + h + 1]
            eg = jnp.exp(aux_ref[b:b + 1, _G0 + h:_G0 + h + 1])
            r = jnp.sum(kcol * s, axis=0, keepdims=True)
            u = beta * (va_ref[b:b + 1, h * A_DV:(h + 1) * A_DV] - eg * r)
            sn = eg * s + kcol * u
            sgn_ref[b, h] = sn
            oa_ref[b:b + 1, h * A_DV:(h + 1) * A_DV] = jnp.sum(qcol * sn, axis=0, keepdims=True)

            s = sl_ref[b, h]
            kcol = kbt[h * B_DK:(h + 1) * B_DK, b:b + 1]
            qcol = qbt[h * B_DK:(h + 1) * B_DK, b:b + 1]
            dcol = decay_t[h * B_DK:(h + 1) * B_DK, b:b + 1]
            sn = dcol * s + kcol * vb_ref[b:b + 1, h * B_DV:(h + 1) * B_DV]
            sln_ref[b, h] = sn
            ob_ref[b:b + 1, h * B_DV:(h + 1) * B_DV] = jnp.sum(qcol * sn, axis=0, keepdims=True)


def _decode_mix(qa, ka, va, aux, qb, kb, vb, lg, sg, sl, *, bt=8):
    n = qa.shape[0]
    row = lambda w: pl.BlockSpec((bt, w), lambda i: (i, 0))
    sgs = pl.BlockSpec((bt, HEADS, A_DK, A_DV), lambda i: (i, 0, 0, 0))
    sls = pl.BlockSpec((bt, HEADS, B_DK, B_DV), lambda i: (i, 0, 0, 0))
    return pl.pallas_call(
        functools.partial(_decode_kernel, bt=bt),
        grid=(n // bt,),
        in_specs=[row(A_QK), row(A_QK), row(A_V), row(SMALL_W), row(B_QK), row(B_QK), row(B_V), row(B_QK), sgs, sls],
        out_specs=[row(A_V), row(B_V), sgs, sls],
        out_shape=[jax.ShapeDtypeStruct((n, A_V), F32), jax.ShapeDtypeStruct((n, B_V), F32),
                   jax.ShapeDtypeStruct(sg.shape, F32), jax.ShapeDtypeStruct(sl.shape, F32)],
        compiler_params=pltpu.CompilerParams(dimension_semantics=("arbitrary",)),
        name="decode_mix",
    )(qa, ka, va, aux, qb, kb, vb, lg, sg, sl)


def _head_rms(o, dv):
    parts = []
    for h in range(HEADS):
        oh = o[:, h * dv:(h + 1) * dv]
        parts.append(oh * lax.rsqrt(jnp.mean(oh * oh, axis=-1, keepdims=True) + 1e-6))
    return jnp.concatenate(parts, axis=-1)


def _mix_kernel(x_ref, oa_ref, ob_ref, wg_ref, na_ref, nb_ref, wa_ref, wb_ref, wo_ref, g_ref, b_ref, out_ref,
                *, alpha):
    x = x_ref[...]
    gates = _dot(_bf(x), wg_ref[...])
    za = gates[:, :A_V]
    rb = gates[:, A_V:A_V + B_V]
    ga = gates[:, A_V + B_V:A_V + B_V + D_MODEL]
    gb = gates[:, A_V + B_V + D_MODEL:]
    oa = _head_rms(oa_ref[...], A_DV) * na_ref[...] * _silu(za)
    ob = _head_rms(ob_ref[...], B_DV) * nb_ref[...] * _silu(rb)
    ya = _dot(_bf(oa), wa_ref[...])
    yb = _dot(_bf(ob), wb_ref[...])
    merged = _sigmoid(ga) * ya + _sigmoid(gb) * yb
    mix = _dot(_bf(merged), wo_ref[...])
    out_ref[...] = _layer_norm(alpha * x + mix, g_ref[...], b_ref[...])


def _mix(x, oa, ob, P, *, tm, alpha):
    n = x.shape[0]
    full = lambda a: pl.BlockSpec(a.shape, lambda i: (0,) * a.ndim)
    row = lambda w: pl.BlockSpec((tm, w), lambda i: (i, 0))
    ws = (P["w_gate"], P["na"], P["nb"], P["wa"], P["wb"], P["wo"], P["ln1g"], P["ln1b"])
    return pl.pallas_call(
        functools.partial(_mix_kernel, alpha=alpha),
        grid=(n // tm,),
        in_specs=[row(D_MODEL), row(A_V), row(B_V)] + [full(a) for a in ws],
        out_specs=row(D_MODEL),
        out_shape=jax.ShapeDtypeStruct((n, D_MODEL), F32),
        compiler_params=pltpu.CompilerParams(dimension_semantics=("arbitrary",), vmem_limit_bytes=VMEM_LIMIT),
        name="mix",
    )(x, oa, ob, *ws)


def _ffn_kernel(x_ref, wi_ref, wo_ref, g_ref, b_ref, out_ref, *, alpha, d_ff, fc):
    x = x_ref[...]
    xb = _bf(x)
    acc = jnp.zeros(x.shape, F32)
    for c in range(d_ff // fc):
        a = _dot(xb, wi_ref[:, c * fc:(c + 1) * fc])
        u = _dot(xb, wi_ref[:, d_ff + c * fc:d_ff + (c + 1) * fc])
        acc = acc + _dot(_bf(_silu(a) * u), wo_ref[c * fc:(c + 1) * fc, :])
    out_ref[...] = _layer_norm(alpha * x + acc, g_ref[...], b_ref[...])


def _ffn(x, P, *, tm, alpha):
    n = x.shape[0]
    d_ff = P["wfo"].shape[0]
    full = lambda a: pl.BlockSpec(a.shape, lambda i: (0,) * a.ndim)
    row = pl.BlockSpec((tm, D_MODEL), lambda i: (i, 0))
    ws = (P["wfi"], P["wfo"], P["ln2g"], P["ln2b"])
    return pl.pallas_call(
        functools.partial(_ffn_kernel, alpha=alpha, d_ff=d_ff, fc=256),
        grid=(n // tm,),
        in_specs=[row] + [full(a) for a in ws],
        out_specs=row,
        out_shape=jax.ShapeDtypeStruct((n, D_MODEL), F32),
        compiler_params=pltpu.CompilerParams(dimension_semantics=("arbitrary",), vmem_limit_bytes=VMEM_LIMIT),
        name="ffn",
    )(x, *ws)


def _layer_params(l, w_in, conv_w, a_log, dt_bias, gdn_norm_w, gla_gate_w2, gla_gate_b, gla_norm_w,
                  w_branch_a, w_branch_b, w_out, ln1_g, ln1_b, ln2_g, ln2_b, w_ffn_in, w_ffn_out):
    sizes = (CONV_CH, A_V, HEADS, HEADS, B_QK, B_QK, B_V, B_V, B_GATE_RANK, D_MODEL, D_MODEL)
    offs = np.concatenate([[0], np.cumsum(sizes)])
    wi = w_in[l]
    col = lambda i: wi[:, offs[i]:offs[i + 1]]
    w_qkv, w_z, w_beta, w_a, w_qb, w_kb, w_vb, w_rb, w_glr, w_ga, w_gb = [col(i) for i in range(len(sizes))]
    pad = jnp.zeros((D_MODEL, SMALL_W - _GLR0 - B_GATE_RANK), F32)
    w_small = jnp.concatenate([w_beta, w_a, w_a, w_glr, pad], axis=1)
    lane_vec = lambda v: jnp.zeros((SMALL_W,), F32).at[_G0:_G0 + HEADS].set(v).at[_GAM0:_GAM0 + HEADS].set(v)
    hp = jnp.zeros((SUBLANES, SMALL_W), F32).at[0].set(lane_vec(a_log[l])).at[1].set(lane_vec(dt_bias[l]))
    w2 = jnp.zeros((SMALL_W, B_QK), F32).at[_GLR0:_GLR0 + B_GATE_RANK].set(gla_gate_w2[l])
    return {
        "w_scan": _bf(jnp.concatenate([w_qkv, w_qb, w_kb, w_vb, w_small], axis=1)),
        "w_gate": _bf(jnp.concatenate([w_z, w_rb, w_ga, w_gb], axis=1)),
        "conv_w": conv_w[l],
        "hp": hp,
        "w2": _bf(w2),
        "gb": gla_gate_b[l][None, :],
        "na": jnp.tile(gdn_norm_w[l], HEADS)[None, :],
        "nb": jnp.tile(gla_norm_w[l], HEADS)[None, :],
        "wa": _bf(w_branch_a[l]),
        "wb": _bf(w_branch_b[l]),
        "wo": _bf(w_out[l]),
        "ln1g": ln1_g[l][None, :],
        "ln1b": ln1_b[l][None, :],
        "ln2g": ln2_g[l][None, :],
        "ln2b": ln2_b[l][None, :],
        "wfi": _bf(w_ffn_in[l]),
        "wfo": _bf(w_ffn_out[l]),
    }


def kernel(x_prompt, x_sample, state_gdn, state_gla, state_conv, meta_tokens, w_in, conv_w, a_log, dt_bias,
           gdn_norm_w, gla_gate_w2, gla_gate_b, gla_norm_w, w_branch_a, w_branch_b, w_out,
           ln1_g, ln1_b, ln2_g, ln2_b, w_ffn_in, w_ffn_out):
    depth = w_in.shape[0]
    alpha = (2.0 * depth) ** 0.25
    nb, seq, _ = x_prompt.shape
    assert seq % CHUNK == 0 and x_sample.shape[1] == 1 and meta_tokens.shape[0] == N_META
    tm_proj = 256
    tm_dense = 256
    bt_scan = nb
    null_rows = CHUNK - N_META

    xp = x_prompt
    xs = x_sample[:, 0, :]
    xm = meta_tokens
    zeros_rows = jnp.zeros((null_rows, D_MODEL), F32)
    outs = {k: [] for k in ("gdn_p", "gla_p", "conv_p", "gdn_s", "gla_s", "conv_s")}
    for l in range(depth):
        P = _layer_params(l, w_in, conv_w, a_log, dt_bias, gdn_norm_w, gla_gate_w2, gla_gate_b, gla_norm_w,
                          w_branch_a, w_branch_b, w_out, ln1_g, ln1_b, ln2_g, ln2_b, w_ffn_in, w_ffn_out)

        xm_pad = jnp.concatenate([zeros_rows, xm], axis=0)[None]
        qa, ka, va, q2, k2, vb, cb2, aux, tail_m = _proj_prompt(
            xm_pad, jnp.zeros((SUBLANES, CONV_CH), F32), P, tm=CHUNK, n_null=null_rows)
        oa_m, sg_m = _gdn_scan(qa, ka, va, aux, jnp.zeros((1, HEADS, A_DK, A_DV), F32), bt=1)
        ob_m, sl_m = _gla_scan(q2, k2, cb2, vb, jnp.zeros((1, HEADS, B_DK, B_DV), F32), bt=1)

        qa, ka, va, q2, k2, vb, cb2, aux, tail_p = _proj_prompt(xp, tail_m[0], P, tm=tm_proj, n_null=0)
        oa_p, sg_p = _gdn_scan(qa, ka, va, aux, sg_m, bt=bt_scan)
        ob_p, sl_p = _gla_scan(q2, k2, cb2, vb, sl_m, bt=bt_scan)
        x1 = _mix(xp.reshape(nb * seq, D_MODEL), oa_p.reshape(nb * seq, A_V), ob_p.reshape(nb * seq, B_V), P,
                  tm=tm_dense, alpha=alpha)
        xp = _ffn(x1, P, tm=tm_dense, alpha=alpha).reshape(nb, seq, D_MODEL)
        outs["gdn_p"].append(sg_p)
        outs["gla_p"].append(sl_p)
        outs["conv_p"].append(tail_p[:, SUBLANES - (CONV_W - 1):, :])

        if l + 1 < depth:
            x1m = _mix(xm, oa_m[0, null_rows:], ob_m[0, null_rows:], P, tm=N_META, alpha=alpha)
            xm = _ffn(x1m, P, tm=N_META, alpha=alpha)

        st = state_conv[l]
        qa, ka, va, qb, kb, vb, lg, aux, u = _proj_decode(xs, st[:, 0], st[:, 1], st[:, 2], P)
        oa_s, ob_s, sg_s, sl_s = _decode_mix(qa, ka, va, aux, qb, kb, vb, lg, state_gdn[l], state_gla[l])
        ns = xs.shape[0]
        xs = _ffn(_mix(xs, oa_s, ob_s, P, tm=ns, alpha=alpha), P, tm=ns, alpha=alpha)
        outs["gdn_s"].append(sg_s)
        outs["gla_s"].append(sl_s)
        outs["conv_s"].append(jnp.stack([st[:, 1], st[:, 2], u], axis=1))

    return (xp, xs[:, None, :], jnp.stack(outs["gdn_p"]), jnp.stack(outs["gla_p"]), jnp.stack(outs["conv_p"]),
            jnp.stack(outs["gdn_s"]), jnp.stack(outs["gla_s"]), jnp.stack(outs["conv_s"]))
```

```python
import functools
import itertools

import numpy as np
import jax
import jax.numpy as jnp
from jax import lax
from jax.experimental import pallas as pl
from jax.experimental.pallas import tpu as pltpu

F32 = jnp.float32
BF16 = jnp.bfloat16

D_MODEL = 1024
N_META = 16
CHUNK = 64
HEADS = 4
A_DK = 128
A_DV = 128
A_QK = HEADS * A_DK
A_V = HEADS * A_DV
CONV_W = 4
CONV_CH = 2 * A_QK + A_V
B_DK = 64
B_DV = 128
B_QK = HEADS * B_DK
B_V = HEADS * B_DV
B_GATE_RANK = 16
B_GATE_TAU = 16.0
SUB = 16
LANES = 128
SUBLANES = 8
SMALL_W = LANES
B_QKV = 2 * B_QK + B_V
SCAN_W = CONV_CH + B_QKV + SMALL_W
GATE_W = A_V + B_V + 2 * D_MODEL
VMEM_LIMIT = 56 * 1024 * 1024

_IN_SIZES = (CONV_CH, A_V, HEADS, HEADS, B_QK, B_QK, B_V, B_V, B_GATE_RANK, D_MODEL, D_MODEL)
(_O_QKV, _O_Z, _O_BETA, _O_A, _O_QB, _O_KB, _O_VB, _O_RB, _O_GLR, _O_GA, _O_GB, D_IN) = (
    int(v) for v in np.concatenate([[0], np.cumsum(_IN_SIZES)]))

_BETA0, _G0, _GAM0, _GLR0 = 0, 4, 8, 12


def _dot(a, b):
    return jnp.dot(a, b, preferred_element_type=F32)


def _dot_nt(a, b):
    return lax.dot_general(a, b, (((1,), (1,)), ((), ())), preferred_element_type=F32)


def _dot_tn(a, b):
    return lax.dot_general(a, b, (((0,), (0,)), ((), ())), preferred_element_type=F32)


def _bf(x):
    return x.astype(BF16)


def _tri_cumsum(ltri, x):
    hi = x.astype(BF16)
    r1 = x - hi.astype(F32)
    mid = r1.astype(BF16)
    lo = (r1 - mid.astype(F32)).astype(BF16)
    return _dot(ltri, hi) + _dot(ltri, mid) + _dot(ltri, lo)


def _sigmoid(x):
    return 1.0 / (1.0 + jnp.exp(-x))


def _silu(x):
    return x * _sigmoid(x)


def _softplus(x):
    return jnp.maximum(x, 0.0) + jnp.log1p(jnp.exp(-jnp.abs(x)))


def _log_sigmoid(x):
    return jnp.minimum(x, 0.0) - jnp.log1p(jnp.exp(-jnp.abs(x)))


def _layer_norm(h, g, b):
    mu = jnp.mean(h, axis=-1, keepdims=True)
    d = h - mu
    var = jnp.mean(d * d, axis=-1, keepdims=True)
    return d * lax.rsqrt(var + 1e-5) * g + b


def _layer_spec(arr, l, single_buffer=False):
    kw = {"pipeline_mode": pl.Buffered(1)} if single_buffer else {}
    zeros = (0,) * (arr.ndim - 1)
    return pl.BlockSpec((None,) + arr.shape[1:], lambda *_: (l,) + zeros, **kw)


def _regroup_scan_weights(w_ref, wsc_ref):
    wsc_ref[:, :CONV_CH] = w_ref[:, _O_QKV:_O_Z]
    wsc_ref[:, CONV_CH:CONV_CH + B_QKV] = w_ref[:, _O_QB:_O_RB]
    s0 = CONV_CH + B_QKV
    wsc_ref[:, s0:s0 + SMALL_W] = jnp.zeros((D_MODEL, SMALL_W), BF16)
    wsc_ref[:, s0 + _BETA0:s0 + _GAM0] = w_ref[:, _O_BETA:_O_QB]
    wsc_ref[:, s0 + _GAM0:s0 + _GLR0] = w_ref[:, _O_A:_O_QB]
    wsc_ref[:, s0 + _GLR0:s0 + _GLR0 + B_GATE_RANK] = w_ref[:, _O_GLR:_O_GA]


def _proj_tail(p, act, hp, w2, gb):
    qs, ks = [], []
    for h in range(HEADS):
        qh = act[:, h * A_DK:(h + 1) * A_DK]
        qs.append(qh * lax.rsqrt(jnp.sum(qh * qh, axis=-1, keepdims=True) + 1e-6) * (A_DK ** -0.5))
        kh = act[:, A_QK + h * A_DK:A_QK + (h + 1) * A_DK]
        ks.append(kh * lax.rsqrt(jnp.sum(kh * kh, axis=-1, keepdims=True) + 1e-6))
    qa = jnp.concatenate(qs, axis=-1)
    ka = jnp.concatenate(ks, axis=-1)
    va = act[:, 2 * A_QK:]
    qb = p[:, CONV_CH:CONV_CH + B_QK] * (B_DK ** -0.5)
    kb = p[:, CONV_CH + B_QK:CONV_CH + 2 * B_QK]
    vb = p[:, CONV_CH + 2 * B_QK:CONV_CH + B_QKV]
    small = p[:, CONV_CH + B_QKV:]
    beta = _sigmoid(small)
    g = -(jnp.exp(hp[0:1, :]) * _softplus(small + hp[1:2, :]))
    lg = _log_sigmoid(_dot(_bf(small), w2) + gb) * (1.0 / B_GATE_TAU)
    return qa, ka, va, qb, kb, vb, beta, g, lg


def _dup_heads(x):
    parts = []
    for h in range(HEADS):
        xh = x[:, h * B_DK:(h + 1) * B_DK]
        parts += [xh, xh]
    return jnp.concatenate(parts, axis=-1)


def _proj_prompt_kernel(x_ref, carry0_ref, w_ref, cw_ref, hp_ref, w2_ref, gb_ref, ltri_ref,
                        qa_ref, ka_ref, va_ref, q2_ref, k2_ref, vb_ref, cb2_ref, aux_ref, tail_ref,
                        wsc_ref, carry_ref, *, tm, n_null):
    @pl.when(jnp.logical_and(pl.program_id(0) == 0, pl.program_id(1) == 0))
    def _():
        _regroup_scan_weights(w_ref, wsc_ref)

    @pl.when(pl.program_id(1) == 0)
    def _():
        carry_ref[...] = carry0_ref[...]

    p = _dot(_bf(x_ref[0]), wsc_ref[...])
    u = p[:, :CONV_CH]
    prev8 = carry_ref[...]
    cw = cw_ref[...]
    row8 = lax.broadcasted_iota(jnp.int32, (SUBLANES, CONV_CH), 0)
    acc = u * cw[CONV_W - 1:CONV_W, :]
    for s in range(1, CONV_W):
        rolled = pltpu.roll(u, s, 0)
        head = jnp.where(row8 < s, pltpu.roll(prev8, s, 0), rolled[:SUBLANES])
        shifted = jnp.concatenate([head, rolled[SUBLANES:]], axis=0)
        acc = acc + shifted * cw[CONV_W - 1 - s:CONV_W - s, :]
    carry_ref[...] = u[tm - SUBLANES:, :]
    tail_ref[0] = u[tm - SUBLANES:, :]

    qa, ka, va, qb, kb, vb, beta, g, lg = _proj_tail(p, _silu(acc), hp_ref[...], w2_ref[...], gb_ref[...])
    if n_null:
        live = lax.broadcasted_iota(jnp.int32, (tm, 1), 0) >= n_null
        g = jnp.where(live, g, 0.0)
        lg = jnp.where(live, lg, 0.0)
    ltri = ltri_ref[...]
    gam = _tri_cumsum(ltri, g)
    cb = _tri_cumsum(ltri, lg)
    lane = lax.broadcasted_iota(jnp.int32, (tm, SMALL_W), 1)
    aux = jnp.where(lane < _G0, beta, jnp.where(lane < _GAM0, g, jnp.where(lane < _GLR0, gam, 0.0)))

    qa_ref[0] = qa
    ka_ref[0] = ka
    va_ref[0] = va
    q2_ref[0] = _dup_heads(qb)
    k2_ref[0] = _dup_heads(kb)
    vb_ref[0] = vb
    cb2_ref[0] = _dup_heads(cb)
    aux_ref[0] = aux


def _proj_prompt(x, carry0, W, l, *, tm, n_null):
    nb, length, _ = x.shape
    blk = np.arange(tm) // CHUNK
    ltri = jnp.asarray((blk[:, None] == blk[None, :]) & (np.arange(tm)[:, None] >= np.arange(tm)[None, :]), BF16)
    full = lambda shape: pl.BlockSpec(shape, lambda b, j: (0,) * len(shape))
    row = lambda w: pl.BlockSpec((1, tm, w), lambda b, j: (b, j, 0))
    out_w = (A_QK, A_QK, A_V, 2 * B_QK, 2 * B_QK, B_V, 2 * B_QK, SMALL_W)
    return pl.pallas_call(
        functools.partial(_proj_prompt_kernel, tm=tm, n_null=n_null),
        grid=(nb, length // tm),
        in_specs=[row(D_MODEL), full((SUBLANES, CONV_CH)), _layer_spec(W["w_in"], l, True),
                  _layer_spec(W["conv_w"], l), _layer_spec(W["hp"], l), _layer_spec(W["w2"], l),
                  _layer_spec(W["gb"], l), full((tm, tm))],
        out_specs=[row(w) for w in out_w] + [pl.BlockSpec((1, SUBLANES, CONV_CH), lambda b, j: (b, 0, 0))],
        out_shape=[jax.ShapeDtypeStruct((nb, length, w), F32) for w in out_w]
        + [jax.ShapeDtypeStruct((nb, SUBLANES, CONV_CH), F32)],
        scratch_shapes=[pltpu.VMEM((D_MODEL, SCAN_W), BF16), pltpu.VMEM((SUBLANES, CONV_CH), F32)],
        compiler_params=pltpu.CompilerParams(dimension_semantics=("arbitrary", "arbitrary"),
                                             vmem_limit_bytes=VMEM_LIMIT),
        name="proj_prompt",
    )(x, carry0, W["w_in"], W["conv_w"], W["hp"], W["w2"], W["gb"], ltri)


def _proj_decode_kernel(x_ref, st_ref, w_ref, cw_ref, hp_ref, w2_ref, gb_ref,
                        qa_ref, ka_ref, va_ref, qb_ref, kb_ref, vb_ref, lg_ref, aux_ref, stn_ref, wsc_ref):
    _regroup_scan_weights(w_ref, wsc_ref)
    p = _dot(_bf(x_ref[...]), wsc_ref[...])
    u = p[:, :CONV_CH]
    cw = cw_ref[...]
    st1 = st_ref[:, 1, :]
    st2 = st_ref[:, 2, :]
    acc = st_ref[:, 0, :] * cw[0:1, :] + st1 * cw[1:2, :] + st2 * cw[2:3, :] + u * cw[3:4, :]
    qa, ka, va, qb, kb, vb, beta, g, lg = _proj_tail(p, _silu(acc), hp_ref[...], w2_ref[...], gb_ref[...])
    lane = lax.broadcasted_iota(jnp.int32, beta.shape, 1)
    qa_ref[...] = qa
    ka_ref[...] = ka
    va_ref[...] = va
    qb_ref[...] = qb
    kb_ref[...] = kb
    vb_ref[...] = vb
    lg_ref[...] = lg
    aux_ref[...] = jnp.where(lane < _G0, beta, jnp.where(lane < _GAM0, g, 0.0))
    stn_ref[:, 0, :] = st1
    stn_ref[:, 1, :] = st2
    stn_ref[:, 2, :] = u


def _proj_decode(x, state_conv, W, l):
    n = x.shape[0]
    out_w = (A_QK, A_QK, A_V, B_QK, B_QK, B_V, B_QK, SMALL_W)
    whole = lambda shape: pl.BlockSpec(shape, lambda i: (0,) * len(shape))
    return pl.pallas_call(
        _proj_decode_kernel,
        grid=(1,),
        in_specs=[whole((n, D_MODEL)), _layer_spec(state_conv, l), _layer_spec(W["w_in"], l, True),
                  _layer_spec(W["conv_w"], l), _layer_spec(W["hp"], l), _layer_spec(W["w2"], l),
                  _layer_spec(W["gb"], l)],
        out_specs=[whole((n, w)) for w in out_w] + [whole((n, CONV_W - 1, CONV_CH))],
        out_shape=[jax.ShapeDtypeStruct((n, w), F32) for w in out_w]
        + [jax.ShapeDtypeStruct((n, CONV_W - 1, CONV_CH), F32)],
        scratch_shapes=[pltpu.VMEM((D_MODEL, SCAN_W), BF16)],
        compiler_params=pltpu.CompilerParams(dimension_semantics=("arbitrary",), vmem_limit_bytes=VMEM_LIMIT),
        name="proj_decode",
    )(x, state_conv, W["w_in"], W["conv_w"], W["hp"], W["w2"], W["gb"])


def _gdn_kernel(q_ref, k_ref, v_ref, aux_ref, auxt_ref, s0_ref, o_ref, sfin_ref, s_ref, *, bt):
    c = pl.program_id(1)

    @pl.when(c == 0)
    def _():
        for b in range(bt):
            s_ref[b] = s0_ref[0]

    ri = lax.broadcasted_iota(jnp.int32, (CHUNK, CHUNK), 0)
    ci = lax.broadcasted_iota(jnp.int32, (CHUNK, CHUNK), 1)
    eye = (ri == ci).astype(F32)
    chains = list(itertools.product(range(bt), range(HEADS)))
    n = len(chains)
    lanes = lambda h: slice(h * A_DK, (h + 1) * A_DK)
    col = lambda b, h, off: aux_ref[b, :, off + h:off + h + 1]
    kb = [_bf(k_ref[b, :, lanes(h)]) for b, h in chains]
    qkk = [_dot_nt(jnp.concatenate([_bf(q_ref[b, :, lanes(h)]), kb[g]], axis=0), kb[g])
           for g, (b, h) in enumerate(chains)]
    dec = [jnp.exp(jnp.where(ri >= ci, col(b, h, _GAM0) - auxt_ref[b, 0, _GAM0 + h:_GAM0 + h + 1, :], -jnp.inf))
           for b, h in chains]
    a = [jnp.where(ri > ci, col(b, h, _BETA0) * dec[g] * qkk[g][CHUNK:], 0.0) for g, (b, h) in enumerate(chains)]
    x = [eye - m for m in a]
    ab = [_bf(m) for m in a]
    y = [_bf(_dot(m, m)) for m in ab]
    steps = int(np.log2(CHUNK)) - 1
    for i in range(steps):
        if i + 1 < steps:
            xy = [_dot(jnp.concatenate([_bf(x[g]), y[g]], axis=0), y[g]) for g in range(n)]
            x = [x[g] + xy[g][:CHUNK] for g in range(n)]
            y = [_bf(xy[g][CHUNK:]) for g in range(n)]
        else:
            x = [x[g] + _dot(_bf(x[g]), y[g]) for g in range(n)]
    eg = [jnp.exp(col(b, h, _GAM0)) for b, h in chains]
    rhs = [_bf(jnp.concatenate([col(b, h, _BETA0) * v_ref[b, :, lanes(h)],
                                (col(b, h, _BETA0) * eg[g]) * k_ref[b, :, lanes(h)]], axis=-1))
           for g, (b, h) in enumerate(chains)]
    sol = [_dot(_bf(m), r) for m, r in zip(x, rhs)]
    ws = [_dot(jnp.concatenate([_bf(sol[g][:, A_DV:]), _bf(q_ref[b, :, lanes(h)] * eg[g])], axis=0),
               _bf(s_ref[b, h])) for g, (b, h) in enumerate(chains)]
    u = [_bf(sol[g][:, :A_DV] - ws[g][:CHUNK]) for g in range(n)]
    for g, (b, h) in enumerate(chains):
        o_ref[b, :, lanes(h)] = ws[g][CHUNK:] + _dot(_bf(qkk[g][:CHUNK] * dec[g]), u[g])
    for g, (b, h) in enumerate(chains):
        glast = auxt_ref[b, 0, _GAM0 + h:_GAM0 + h + 1, CHUNK - 1:CHUNK]
        kd = _bf(k_ref[b, :, lanes(h)] * jnp.exp(glast - col(b, h, _GAM0)))
        s_ref[b, h] = jnp.exp(glast) * s_ref[b, h] + _dot_tn(kd, u[g])

    @pl.when(c == pl.num_programs(1) - 1)
    def _():
        sfin_ref[...] = s_ref[...]


def _gdn_scan(qa, ka, va, aux, s0, *, bt):
    nb, length, _ = qa.shape
    nc = length // CHUNK
    auxt = aux[:, :, :2 * SUBLANES].reshape(nb, nc, CHUNK, 2 * SUBLANES).transpose(0, 1, 3, 2)
    row = lambda w: pl.BlockSpec((bt, CHUNK, w), lambda b, c: (b, c, 0))
    return pl.pallas_call(
        functools.partial(_gdn_kernel, bt=bt),
        grid=(nb // bt, nc),
        in_specs=[row(A_QK), row(A_QK), row(A_V), row(SMALL_W),
                  pl.BlockSpec((bt, 1, 2 * SUBLANES, CHUNK), lambda b, c: (b, c, 0, 0)),
                  pl.BlockSpec((1, HEADS, A_DK, A_DV), lambda b, c: (0, 0, 0, 0))],
        out_specs=[row(A_V), pl.BlockSpec((bt, HEADS, A_DK, A_DV), lambda b, c: (b, 0, 0, 0))],
        out_shape=[jax.ShapeDtypeStruct((nb, length, A_V), F32),
                   jax.ShapeDtypeStruct((nb, HEADS, A_DK, A_DV), F32)],
        scratch_shapes=[pltpu.VMEM((bt, HEADS, A_DK, A_DV), F32)],
        compiler_params=pltpu.CompilerParams(dimension_semantics=("arbitrary", "arbitrary")),
        name="gdn_scan",
    )(qa, ka, va, aux, auxt, s0)


def _gla_kernel(q2_ref, k2_ref, c2_ref, v_ref, seg_ref, s0_ref, o_ref, sfin_ref, s_ref, *, bt):
    c = pl.program_id(1)

    @pl.when(c == 0)
    def _():
        for b in range(bt):
            s_ref[b] = s0_ref[0]

    ri = lax.broadcasted_iota(jnp.int32, (CHUNK, CHUNK), 0)
    ci = lax.broadcasted_iota(jnp.int32, (CHUNK, CHUNK), 1)
    rblk = ri // SUB
    cblk = ci // SUB
    r16 = lax.broadcasted_iota(jnp.int32, (SUB, LANES), 0)
    l16 = lax.broadcasted_iota(jnp.int32, (SUB, LANES), 1)
    upper = (l16 >= B_DK).astype(jnp.int32)
    low1 = lax.broadcasted_iota(jnp.int32, (1, LANES), 1) < B_DK
    nsub = CHUNK // SUB
    for b, h in itertools.product(range(bt), range(HEADS)):
        lanes2 = slice(h * LANES, (h + 1) * LANES)
        lanes1 = slice(h * LANES, h * LANES + B_DK)
        q = q2_ref[b, :, lanes1]
        k = k2_ref[b, :, lanes1]
        cb = c2_ref[b, :, lanes1]
        v = v_ref[b, :, h * B_DV:(h + 1) * B_DV]

        zrows = []
        for i in range(nsub):
            rows = slice(i * SUB, (i + 1) * SUB)
            qq = q2_ref[b, rows, lanes2]
            cc = c2_ref[b, rows, lanes2]
            pieces = []
            for m in range(SUB // 2):
                r0 = i * SUB + 2 * m
                kp = jnp.where(low1, k2_ref[b, r0:r0 + 1, lanes2], k2_ref[b, r0 + 1:r0 + 2, lanes2])
                cp = jnp.where(low1, c2_ref[b, r0:r0 + 1, lanes2], c2_ref[b, r0 + 1:r0 + 2, lanes2])
                z = qq * kp * jnp.exp(jnp.minimum(cc - cp, 0.0))
                pieces.append(jnp.where(r16 >= 2 * m + upper, z, 0.0))
            zrows.append(jnp.concatenate(pieces, axis=-1))
        diag = _dot(_bf(jnp.concatenate(zrows, axis=0)), seg_ref[...])

        offs = [jnp.zeros((SUB, CHUNK), F32)]
        for i in range(1, nsub):
            rows = slice(i * SUB, (i + 1) * SUB)
            ref_row = c2_ref[b, i * SUB:i * SUB + 1, lanes1]
            qt = q[rows] * jnp.exp(cb[rows] - ref_row)
            kt = k * jnp.exp(jnp.minimum(ref_row - cb, 0.0))
            offs.append(_dot_nt(_bf(qt), _bf(kt)))
        off = jnp.concatenate(offs, axis=0)
        attn = jnp.where(cblk < rblk, off, jnp.where(cblk == rblk, diag, 0.0))

        s = s_ref[b, h]
        vb16 = _bf(v)
        o_ref[b, :, h * B_DV:(h + 1) * B_DV] = _dot(_bf(q * jnp.exp(cb)), _bf(s)) + _dot(_bf(attn), vb16)
        kt_ = k.T
        cbt = cb.T
        clast = cbt[:, CHUNK - 1:CHUNK]
        s_ref[b, h] = jnp.exp(clast) * s + _dot(_bf(kt_ * jnp.exp(clast - cbt)), vb16)

    @pl.when(c == pl.num_programs(1) - 1)
    def _():
        sfin_ref[...] = s_ref[...]


def _gla_scan(q2, k2, cb2, vb, s0, *, bt):
    nb, length, _ = q2.shape
    nc = length // CHUNK
    rows = np.arange(SUB // 2 * LANES)
    jj = 2 * (rows // LANES) + (rows % LANES) // B_DK
    seg = jnp.asarray(jj[:, None] == (np.arange(CHUNK)[None, :] % SUB), BF16)
    row = lambda w: pl.BlockSpec((bt, CHUNK, w), lambda b, c: (b, c, 0))
    return pl.pallas_call(
        functools.partial(_gla_kernel, bt=bt),
        grid=(nb // bt, nc),
        in_specs=[row(2 * B_QK), row(2 * B_QK), row(2 * B_QK), row(B_V),
                  pl.BlockSpec(seg.shape, lambda b, c: (0, 0)),
                  pl.BlockSpec((1, HEADS, B_DK, B_DV), lambda b, c: (0, 0, 0, 0))],
        out_specs=[row(B_V), pl.BlockSpec((bt, HEADS, B_DK, B_DV), lambda b, c: (b, 0, 0, 0))],
        out_shape=[jax.ShapeDtypeStruct((nb, length, B_V), F32),
                   jax.ShapeDtypeStruct((nb, HEADS, B_DK, B_DV), F32)],
        scratch_shapes=[pltpu.VMEM((bt, HEADS, B_DK, B_DV), F32)],
        compiler_params=pltpu.CompilerParams(dimension_semantics=("arbitrary", "arbitrary")),
        name="gla_scan",
    )(q2, k2, cb2, vb, seg, s0)


def _decode_kernel(qa_ref, ka_ref, va_ref, aux_ref, qb_ref, kb_ref, vb_ref, lg_ref, sg_ref, sl_ref,
                   oa_ref, ob_ref, sgn_ref, sln_ref, *, bt):
    qat = qa_ref[...].T
    kat = ka_ref[...].T
    qbt = qb_ref[...].T
    kbt = kb_ref[...].T
    decay_t = jnp.exp(lg_ref[...]).T
    for b in range(bt):
        for h in range(HEADS):
            s = sg_ref[b, h]
            kcol = kat[h * A_DK:(h + 1) * A_DK, b:b + 1]
            qcol = qat[h * A_DK:(h + 1) * A_DK, b:b + 1]
            beta = aux_ref[b:b + 1, _BETA0 + h:_BETA0 + h + 1]
            eg = jnp.exp(aux_ref[b:b + 1, _G0 + h:_G0 + h + 1])
            r = jnp.sum(kcol * s, axis=0, keepdims=True)
            u = beta * (va_ref[b:b + 1, h * A_DV:(h + 1) * A_DV] - eg * r)
            sn = eg * s + kcol * u
            sgn_ref[b, h] = sn
            oa_ref[b:b + 1, h * A_DV:(h + 1) * A_DV] = jnp.sum(qcol * sn, axis=0, keepdims=True)

            s = sl_ref[b, h]
            kcol = kbt[h * B_DK:(h + 1) * B_DK, b:b + 1]
            qcol = qbt[h * B_DK:(h + 1) * B_DK, b:b + 1]
            dcol = decay_t[h * B_DK:(h + 1) * B_DK, b:b + 1]
            sn = dcol * s + kcol * vb_ref[b:b + 1, h * B_DV:(h + 1) * B_DV]
            sln_ref[b, h] = sn
            ob_ref[b:b + 1, h * B_DV:(h + 1) * B_DV] = jnp.sum(qcol * sn, axis=0, keepdims=True)


def _decode_mix(qa, ka, va, aux, qb, kb, vb, lg, state_gdn, state_gla, l, *, bt=8):
    n = qa.shape[0]
    row = lambda w: pl.BlockSpec((bt, w), lambda i: (i, 0))
    sg_in = pl.BlockSpec((None, bt, HEADS, A_DK, A_DV), lambda i: (l, i, 0, 0, 0))
    sl_in = pl.BlockSpec((None, bt, HEADS, B_DK, B_DV), lambda i: (l, i, 0, 0, 0))
    sgs = pl.BlockSpec((bt, HEADS, A_DK, A_DV), lambda i: (i, 0, 0, 0))
    sls = pl.BlockSpec((bt, HEADS, B_DK, B_DV), lambda i: (i, 0, 0, 0))
    return pl.pallas_call(
        functools.partial(_decode_kernel, bt=bt),
        grid=(n // bt,),
        in_specs=[row(A_QK), row(A_QK), row(A_V), row(SMALL_W), row(B_QK), row(B_QK), row(B_V), row(B_QK),
                  sg_in, sl_in],
        out_specs=[row(A_V), row(B_V), sgs, sls],
        out_shape=[jax.ShapeDtypeStruct((n, A_V), F32), jax.ShapeDtypeStruct((n, B_V), F32),
                   jax.ShapeDtypeStruct(state_gdn.shape[1:], F32), jax.ShapeDtypeStruct(state_gla.shape[1:], F32)],
        compiler_params=pltpu.CompilerParams(dimension_semantics=("arbitrary",)),
        name="decode_mix",
    )(qa, ka, va, aux, qb, kb, vb, lg, state_gdn, state_gla)


def _head_rms(o, dv):
    parts = []
    for h in range(HEADS):
        oh = o[:, h * dv:(h + 1) * dv]
        parts.append(oh * lax.rsqrt(jnp.mean(oh * oh, axis=-1, keepdims=True) + 1e-6))
    return jnp.concatenate(parts, axis=-1)


def _mix_kernel(x_ref, oa_ref, ob_ref, w_ref, na_ref, nb_ref, wa_ref, wb_ref, wo_ref, g_ref, b_ref, out_ref,
                wg_ref, *, alpha):
    @pl.when(pl.program_id(0) == 0)
    def _():
        wg_ref[:, :A_V] = w_ref[:, _O_Z:_O_BETA]
        wg_ref[:, A_V:A_V + B_V] = w_ref[:, _O_RB:_O_GLR]
        wg_ref[:, A_V + B_V:] = w_ref[:, _O_GA:D_IN]

    x = x_ref[...]
    gates = _dot(_bf(x), wg_ref[...])
    za = gates[:, :A_V]
    rb = gates[:, A_V:A_V + B_V]
    ga = gates[:, A_V + B_V:A_V + B_V + D_MODEL]
    gb = gates[:, A_V + B_V + D_MODEL:]
    oa = _head_rms(oa_ref[...], A_DV) * na_ref[...] * _silu(za)
    ob = _head_rms(ob_ref[...], B_DV) * nb_ref[...] * _silu(rb)
    ya = _dot(_bf(oa), wa_ref[...])
    yb = _dot(_bf(ob), wb_ref[...])
    merged = _sigmoid(ga) * ya + _sigmoid(gb) * yb
    mix = _dot(_bf(merged), wo_ref[...])
    out_ref[...] = _layer_norm(alpha * x + mix, g_ref[...], b_ref[...])


def _mix(x, oa, ob, W, l, *, tm, alpha):
    n = x.shape[0]
    row = lambda w: pl.BlockSpec((tm, w), lambda i: (i, 0))
    names = ("w_in", "na", "nb", "wa", "wb", "wo", "ln1g", "ln1b")
    return pl.pallas_call(
        functools.partial(_mix_kernel, alpha=alpha),
        grid=(n // tm,),
        in_specs=[row(D_MODEL), row(A_V), row(B_V)] + [_layer_spec(W[k], l, W[k].dtype == BF16) for k in names],
        out_specs=row(D_MODEL),
        out_shape=jax.ShapeDtypeStruct((n, D_MODEL), F32),
        scratch_shapes=[pltpu.VMEM((D_MODEL, GATE_W), BF16)],
        compiler_params=pltpu.CompilerParams(dimension_semantics=("arbitrary",), vmem_limit_bytes=VMEM_LIMIT),
        name="mix",
    )(x, oa, ob, *[W[k] for k in names])


def _ffn_kernel(x_ref, wi_ref, wo_ref, g_ref, b_ref, out_ref, *, alpha, d_ff, fc):
    x = x_ref[...]
    xb = _bf(x)
    acc = jnp.zeros(x.shape, F32)
    for c in range(d_ff // fc):
        a = _dot(xb, wi_ref[:, c * fc:(c + 1) * fc])
        u = _dot(xb, wi_ref[:, d_ff + c * fc:d_ff + (c + 1) * fc])
        acc = acc + _dot(_bf(_silu(a) * u), wo_ref[c * fc:(c + 1) * fc, :])
    out_ref[...] = _layer_norm(alpha * x + acc, g_ref[...], b_ref[...])


def _ffn(x, W, l, *, tm, alpha):
    n = x.shape[0]
    d_ff = W["wfo"].shape[1]
    row = pl.BlockSpec((tm, D_MODEL), lambda i: (i, 0))
    names = ("wfi", "wfo", "ln2g", "ln2b")
    return pl.pallas_call(
        functools.partial(_ffn_kernel, alpha=alpha, d_ff=d_ff, fc=256),
        grid=(n // tm,),
        in_specs=[row] + [_layer_spec(W[k], l, W[k].dtype == BF16) for k in names],
        out_specs=row,
        out_shape=jax.ShapeDtypeStruct((n, D_MODEL), F32),
        compiler_params=pltpu.CompilerParams(dimension_semantics=("arbitrary",), vmem_limit_bytes=VMEM_LIMIT),
        name="ffn",
    )(x, *[W[k] for k in names])


def _stacked_params(w_in, conv_w, a_log, dt_bias, gdn_norm_w, gla_gate_w2, gla_gate_b, gla_norm_w,
                    w_branch_a, w_branch_b, w_out, ln1_g, ln1_b, ln2_g, ln2_b, w_ffn_in, w_ffn_out):
    lane_row = lambda v: jnp.pad(jnp.concatenate([v, v], axis=-1), ((0, 0), (_G0, SMALL_W - _GLR0)))
    hp = jnp.pad(jnp.stack([lane_row(a_log), lane_row(dt_bias)], axis=1), ((0, 0), (0, SUBLANES - 2), (0, 0)))
    w2 = jnp.pad(gla_gate_w2, ((0, 0), (_GLR0, SMALL_W - _GLR0 - B_GATE_RANK), (0, 0)))
    return {
        "w_in": _bf(w_in), "conv_w": conv_w, "hp": hp, "w2": _bf(w2), "gb": gla_gate_b[:, None, :],
        "na": jnp.tile(gdn_norm_w, (1, HEADS))[:, None, :], "nb": jnp.tile(gla_norm_w, (1, HEADS))[:, None, :],
        "wa": _bf(w_branch_a), "wb": _bf(w_branch_b), "wo": _bf(w_out),
        "ln1g": ln1_g[:, None, :], "ln1b": ln1_b[:, None, :], "ln2g": ln2_g[:, None, :], "ln2b": ln2_b[:, None, :],
        "wfi": _bf(w_ffn_in), "wfo": _bf(w_ffn_out),
    }


def kernel(x_prompt, x_sample, state_gdn, state_gla, state_conv, meta_tokens, w_in, conv_w, a_log, dt_bias,
           gdn_norm_w, gla_gate_w2, gla_gate_b, gla_norm_w, w_branch_a, w_branch_b, w_out,
           ln1_g, ln1_b, ln2_g, ln2_b, w_ffn_in, w_ffn_out):
    depth = w_in.shape[0]
    alpha = (2.0 * depth) ** 0.25
    nb, seq, _ = x_prompt.shape
    assert seq % CHUNK == 0 and x_sample.shape[1] == 1 and meta_tokens.shape[0] == N_META
    assert w_in.shape[2] == D_IN
    tm_proj = 256
    tm_mix = 256
    tm_ffn = 512
    bt_scan = nb
    null_rows = CHUNK - N_META
    W = _stacked_params(w_in, conv_w, a_log, dt_bias, gdn_norm_w, gla_gate_w2, gla_gate_b, gla_norm_w,
                        w_branch_a, w_branch_b, w_out, ln1_g, ln1_b, ln2_g, ln2_b, w_ffn_in, w_ffn_out)

    xp = x_prompt
    xs = x_sample[:, 0, :]
    ns = xs.shape[0]
    xm = meta_tokens
    zeros_rows = jnp.zeros((null_rows, D_MODEL), F32)
    zero_tail = jnp.zeros((SUBLANES, CONV_CH), F32)
    zero_gdn = jnp.zeros((1, HEADS, A_DK, A_DV), F32)
    zero_gla = jnp.zeros((1, HEADS, B_DK, B_DV), F32)
    outs = {k: [] for k in ("gdn_p", "gla_p", "conv_p", "gdn_s", "gla_s", "conv_s")}
    for l in range(depth):
        xm_pad = jnp.concatenate([zeros_rows, xm], axis=0)[None]
        qa, ka, va, q2, k2, vb, cb2, aux, tail_m = _proj_prompt(xm_pad, zero_tail, W, l, tm=CHUNK, n_null=null_rows)
        oa_m, sg_m = _gdn_scan(qa, ka, va, aux, zero_gdn, bt=1)
        ob_m, sl_m = _gla_scan(q2, k2, cb2, vb, zero_gla, bt=1)

        qa, ka, va, q2, k2, vb, cb2, aux, tail_p = _proj_prompt(xp, tail_m[0], W, l, tm=tm_proj, n_null=0)
        oa_p, sg_p = _gdn_scan(qa, ka, va, aux, sg_m, bt=bt_scan)
        ob_p, sl_p = _gla_scan(q2, k2, cb2, vb, sl_m, bt=bt_scan)
        x1 = _mix(xp.reshape(nb * seq, D_MODEL), oa_p.reshape(nb * seq, A_V), ob_p.reshape(nb * seq, B_V), W, l,
                  tm=tm_mix, alpha=alpha)
        xp = _ffn(x1, W, l, tm=tm_ffn, alpha=alpha).reshape(nb, seq, D_MODEL)
        outs["gdn_p"].append(sg_p)
        outs["gla_p"].append(sl_p)
        outs["conv_p"].append(tail_p[:, SUBLANES - (CONV_W - 1):, :])

        if l + 1 < depth:
            x1m = _mix(xm, oa_m[0, null_rows:], ob_m[0, null_rows:], W, l, tm=N_META, alpha=alpha)
            xm = _ffn(x1m, W, l, tm=N_META, alpha=alpha)

        qa, ka, va, qb, kb, vb, lg, aux, conv_s = _proj_decode(xs, state_conv, W, l)
        oa_s, ob_s, sg_s, sl_s = _decode_mix(qa, ka, va, aux, qb, kb, vb, lg, state_gdn, state_gla, l)
        xs = _ffn(_mix(xs, oa_s, ob_s, W, l, tm=ns, alpha=alpha), W, l, tm=ns, alpha=alpha)
        outs["gdn_s"].append(sg_s)
        outs["gla_s"].append(sl_s)
        outs["conv_s"].append(conv_s)

    return (xp, xs[:, None, :], jnp.stack(outs["gdn_p"]), jnp.stack(outs["gla_p"]), jnp.stack(outs["conv_p"]),
            jnp.stack(outs["gdn_s"]), jnp.stack(outs["gla_s"]), jnp.stack(outs["conv_s"]))
```

```python
import functools
import itertools

import numpy as np
import jax
import jax.numpy as jnp
from jax import lax
from jax.experimental import pallas as pl
from jax.experimental.pallas import tpu as pltpu

F32 = jnp.float32
BF16 = jnp.bfloat16

D_MODEL = 1024
N_META = 16
CHUNK = 64
HEADS = 4
A_DK = 128
A_DV = 128
A_QK = HEADS * A_DK
A_V = HEADS * A_DV
CONV_W = 4
CONV_CH = 2 * A_QK + A_V
B_DK = 64
B_DV = 128
B_QK = HEADS * B_DK
B_V = HEADS * B_DV
B_GATE_RANK = 16
B_GATE_TAU = 16.0
LOG2E = 1.4426950408889634
LANES = 128
SUBLANES = 8
SMALL_W = LANES
B_QKV = 2 * B_QK + B_V
SCAN_W = CONV_CH + B_QKV + SMALL_W
GATE_W = A_V + B_V + 2 * D_MODEL
VMEM_LIMIT = 56 * 1024 * 1024

_IN_SIZES = (CONV_CH, A_V, HEADS, HEADS, B_QK, B_QK, B_V, B_V, B_GATE_RANK, D_MODEL, D_MODEL)
(_O_QKV, _O_Z, _O_BETA, _O_A, _O_QB, _O_KB, _O_VB, _O_RB, _O_GLR, _O_GA, _O_GB, D_IN) = (
    int(v) for v in np.concatenate([[0], np.cumsum(_IN_SIZES)]))

_BETA0, _G0, _GAM0, _GLR0 = 0, 4, 8, 12


def _dot(a, b):
    return jnp.dot(a, b, preferred_element_type=F32)


def _dot_nt(a, b):
    return lax.dot_general(a, b, (((1,), (1,)), ((), ())), preferred_element_type=F32)


def _dot_tn(a, b):
    return lax.dot_general(a, b, (((0,), (0,)), ((), ())), preferred_element_type=F32)


def _bf(x):
    return x.astype(BF16)


def _tri_cumsum(ltri, x):
    hi = x.astype(BF16)
    r1 = x - hi.astype(F32)
    mid = r1.astype(BF16)
    lo = (r1 - mid.astype(F32)).astype(BF16)
    return _dot(ltri, hi) + _dot(ltri, mid) + _dot(ltri, lo)


def _sigmoid(x):
    return 1.0 / (1.0 + jnp.exp(-x))


def _silu(x):
    return x * _sigmoid(x)


def _softplus(x):
    return jnp.maximum(x, 0.0) + jnp.log1p(jnp.exp(-jnp.abs(x)))


def _log_sigmoid(x):
    return jnp.minimum(x, 0.0) - jnp.log1p(jnp.exp(-jnp.abs(x)))


def _layer_norm(h, g, b):
    mu = jnp.mean(h, axis=-1, keepdims=True)
    d = h - mu
    var = jnp.mean(d * d, axis=-1, keepdims=True)
    return d * lax.rsqrt(var + 1e-5) * g + b


def _layer_spec(arr, l, single_buffer=False):
    kw = {"pipeline_mode": pl.Buffered(1)} if single_buffer else {}
    zeros = (0,) * (arr.ndim - 1)
    return pl.BlockSpec((None,) + arr.shape[1:], lambda *_: (l,) + zeros, **kw)


def _regroup_scan_weights(w_ref, wsc_ref):
    wsc_ref[:, :CONV_CH] = w_ref[:, _O_QKV:_O_Z]
    wsc_ref[:, CONV_CH:CONV_CH + B_QKV] = w_ref[:, _O_QB:_O_RB]
    s0 = CONV_CH + B_QKV
    wsc_ref[:, s0:s0 + SMALL_W] = jnp.zeros((D_MODEL, SMALL_W), BF16)
    wsc_ref[:, s0 + _BETA0:s0 + _GAM0] = w_ref[:, _O_BETA:_O_QB]
    wsc_ref[:, s0 + _GAM0:s0 + _GLR0] = w_ref[:, _O_A:_O_QB]
    wsc_ref[:, s0 + _GLR0:s0 + _GLR0 + B_GATE_RANK] = w_ref[:, _O_GLR:_O_GA]


def _proj_tail(p, act, hp, w2, gb):
    qs, ks = [], []
    for h in range(HEADS):
        qh = act[:, h * A_DK:(h + 1) * A_DK]
        qs.append(qh * lax.rsqrt(jnp.sum(qh * qh, axis=-1, keepdims=True) + 1e-6) * (A_DK ** -0.5))
        kh = act[:, A_QK + h * A_DK:A_QK + (h + 1) * A_DK]
        ks.append(kh * lax.rsqrt(jnp.sum(kh * kh, axis=-1, keepdims=True) + 1e-6))
    qa = jnp.concatenate(qs, axis=-1)
    ka = jnp.concatenate(ks, axis=-1)
    va = act[:, 2 * A_QK:]
    qb = p[:, CONV_CH:CONV_CH + B_QK] * (B_DK ** -0.5)
    kb = p[:, CONV_CH + B_QK:CONV_CH + 2 * B_QK]
    vb = p[:, CONV_CH + 2 * B_QK:CONV_CH + B_QKV]
    small = p[:, CONV_CH + B_QKV:]
    beta = _sigmoid(small)
    g = -(jnp.exp(hp[0:1, :]) * _softplus(small + hp[1:2, :]))
    lg = _log_sigmoid(_dot(_bf(small), w2) + gb) * (1.0 / B_GATE_TAU)
    return qa, ka, va, qb, kb, vb, beta, g, lg


def _proj_prompt_kernel(x_ref, carry0_ref, w_ref, cw_ref, hp_ref, w2_ref, gb_ref, ltri_ref,
                        qa_ref, ka_ref, va_ref, qb_ref, kb_ref, vb_ref, cb_ref, aux_ref, tail_ref,
                        wsc_ref, carry_ref, *, tm, n_null):
    @pl.when(jnp.logical_and(pl.program_id(0) == 0, pl.program_id(1) == 0))
    def _():
        _regroup_scan_weights(w_ref, wsc_ref)

    @pl.when(pl.program_id(1) == 0)
    def _():
        carry_ref[...] = carry0_ref[...]

    p = _dot(_bf(x_ref[0]), wsc_ref[...])
    u = p[:, :CONV_CH]
    prev8 = carry_ref[...]
    cw = cw_ref[...]
    row8 = lax.broadcasted_iota(jnp.int32, (SUBLANES, CONV_CH), 0)
    acc = u * cw[CONV_W - 1:CONV_W, :]
    for s in range(1, CONV_W):
        rolled = pltpu.roll(u, s, 0)
        head = jnp.where(row8 < s, pltpu.roll(prev8, s, 0), rolled[:SUBLANES])
        shifted = jnp.concatenate([head, rolled[SUBLANES:]], axis=0)
        acc = acc + shifted * cw[CONV_W - 1 - s:CONV_W - s, :]
    carry_ref[...] = u[tm - SUBLANES:, :]
    tail_ref[0] = u[tm - SUBLANES:, :]

    qa, ka, va, qb, kb, vb, beta, g, lg = _proj_tail(p, _silu(acc), hp_ref[...], w2_ref[...], gb_ref[...])
    if n_null:
        live = lax.broadcasted_iota(jnp.int32, (tm, 1), 0) >= n_null
        g = jnp.where(live, g, 0.0)
        lg = jnp.where(live, lg, 0.0)
    ltri = ltri_ref[...]
    gam = _tri_cumsum(ltri, g)
    cb = _tri_cumsum(ltri, lg * LOG2E)
    lane = lax.broadcasted_iota(jnp.int32, (tm, SMALL_W), 1)
    aux = jnp.where(lane < _G0, beta, jnp.where(lane < _GAM0, g, jnp.where(lane < _GLR0, gam, 0.0)))

    qa_ref[0] = qa
    ka_ref[0] = ka
    va_ref[0] = va
    qb_ref[0] = qb
    kb_ref[0] = kb
    vb_ref[0] = vb
    cb_ref[0] = cb
    aux_ref[0] = aux


def _proj_prompt(x, carry0, W, l, *, tm, n_null):
    nb, length, _ = x.shape
    blk = np.arange(tm) // CHUNK
    ltri = jnp.asarray((blk[:, None] == blk[None, :]) & (np.arange(tm)[:, None] >= np.arange(tm)[None, :]), BF16)
    full = lambda shape: pl.BlockSpec(shape, lambda b, j: (0,) * len(shape))
    row = lambda w: pl.BlockSpec((1, tm, w), lambda b, j: (b, j, 0))
    out_w = (A_QK, A_QK, A_V, B_QK, B_QK, B_V, B_QK, SMALL_W)
    return pl.pallas_call(
        functools.partial(_proj_prompt_kernel, tm=tm, n_null=n_null),
        grid=(nb, length // tm),
        in_specs=[row(D_MODEL), full((SUBLANES, CONV_CH)), _layer_spec(W["w_in"], l, True),
                  _layer_spec(W["conv_w"], l), _layer_spec(W["hp"], l), _layer_spec(W["w2"], l),
                  _layer_spec(W["gb"], l), full((tm, tm))],
        out_specs=[row(w) for w in out_w] + [pl.BlockSpec((1, SUBLANES, CONV_CH), lambda b, j: (b, 0, 0))],
        out_shape=[jax.ShapeDtypeStruct((nb, length, w), F32) for w in out_w]
        + [jax.ShapeDtypeStruct((nb, SUBLANES, CONV_CH), F32)],
        scratch_shapes=[pltpu.VMEM((D_MODEL, SCAN_W), BF16), pltpu.VMEM((SUBLANES, CONV_CH), F32)],
        compiler_params=pltpu.CompilerParams(dimension_semantics=("arbitrary", "arbitrary"),
                                             vmem_limit_bytes=VMEM_LIMIT),
        name="proj_prompt",
    )(x, carry0, W["w_in"], W["conv_w"], W["hp"], W["w2"], W["gb"], ltri)


def _proj_decode_kernel(x_ref, st_ref, w_ref, cw_ref, hp_ref, w2_ref, gb_ref,
                        qa_ref, ka_ref, va_ref, qb_ref, kb_ref, vb_ref, lg_ref, aux_ref, stn_ref, wsc_ref):
    _regroup_scan_weights(w_ref, wsc_ref)
    p = _dot(_bf(x_ref[...]), wsc_ref[...])
    u = p[:, :CONV_CH]
    cw = cw_ref[...]
    st1 = st_ref[:, 1, :]
    st2 = st_ref[:, 2, :]
    acc = st_ref[:, 0, :] * cw[0:1, :] + st1 * cw[1:2, :] + st2 * cw[2:3, :] + u * cw[3:4, :]
    qa, ka, va, qb, kb, vb, beta, g, lg = _proj_tail(p, _silu(acc), hp_ref[...], w2_ref[...], gb_ref[...])
    lane = lax.broadcasted_iota(jnp.int32, beta.shape, 1)
    qa_ref[...] = qa
    ka_ref[...] = ka
    va_ref[...] = va
    qb_ref[...] = qb
    kb_ref[...] = kb
    vb_ref[...] = vb
    lg_ref[...] = lg
    aux_ref[...] = jnp.where(lane < _G0, beta, jnp.where(lane < _GAM0, g, 0.0))
    stn_ref[:, 0, :] = st1
    stn_ref[:, 1, :] = st2
    stn_ref[:, 2, :] = u


def _proj_decode(x, state_conv, W, l):
    n = x.shape[0]
    out_w = (A_QK, A_QK, A_V, B_QK, B_QK, B_V, B_QK, SMALL_W)
    whole = lambda shape: pl.BlockSpec(shape, lambda i: (0,) * len(shape))
    return pl.pallas_call(
        _proj_decode_kernel,
        grid=(1,),
        in_specs=[whole((n, D_MODEL)), _layer_spec(state_conv, l), _layer_spec(W["w_in"], l, True),
                  _layer_spec(W["conv_w"], l), _layer_spec(W["hp"], l), _layer_spec(W["w2"], l),
                  _layer_spec(W["gb"], l)],
        out_specs=[whole((n, w)) for w in out_w] + [whole((n, CONV_W - 1, CONV_CH))],
        out_shape=[jax.ShapeDtypeStruct((n, w), F32) for w in out_w]
        + [jax.ShapeDtypeStruct((n, CONV_W - 1, CONV_CH), F32)],
        scratch_shapes=[pltpu.VMEM((D_MODEL, SCAN_W), BF16)],
        compiler_params=pltpu.CompilerParams(dimension_semantics=("arbitrary",), vmem_limit_bytes=VMEM_LIMIT),
        name="proj_decode",
    )(x, state_conv, W["w_in"], W["conv_w"], W["hp"], W["w2"], W["gb"])


def _gdn_kernel(q_ref, k_ref, v_ref, aux_ref, auxt_ref, s0_ref, o_ref, sfin_ref, s_ref, *, bt):
    c = pl.program_id(1)

    @pl.when(c == 0)
    def _():
        for b in range(bt):
            s_ref[b] = s0_ref[0]

    ri = lax.broadcasted_iota(jnp.int32, (CHUNK, CHUNK), 0)
    ci = lax.broadcasted_iota(jnp.int32, (CHUNK, CHUNK), 1)
    eye = (ri == ci).astype(F32)
    chains = list(itertools.product(range(bt), range(HEADS)))
    n = len(chains)
    lanes = lambda h: slice(h * A_DK, (h + 1) * A_DK)
    col = lambda b, h, off: aux_ref[b, :, off + h:off + h + 1]
    kb = [_bf(k_ref[b, :, lanes(h)]) for b, h in chains]
    qkk = [_dot_nt(jnp.concatenate([_bf(q_ref[b, :, lanes(h)]), kb[g]], axis=0), kb[g])
           for g, (b, h) in enumerate(chains)]
    dec = [jnp.exp(jnp.where(ri >= ci, col(b, h, _GAM0) - auxt_ref[b, 0, _GAM0 + h:_GAM0 + h + 1, :], -jnp.inf))
           for b, h in chains]
    a = [jnp.where(ri > ci, col(b, h, _BETA0) * dec[g] * qkk[g][CHUNK:], 0.0) for g, (b, h) in enumerate(chains)]
    x = [eye - m for m in a]
    ab = [_bf(m) for m in a]
    y = [_bf(_dot(m, m)) for m in ab]
    steps = int(np.log2(CHUNK)) - 1
    for i in range(steps):
        if i + 1 < steps:
            xy = [_dot(jnp.concatenate([_bf(x[g]), y[g]], axis=0), y[g]) for g in range(n)]
            x = [x[g] + xy[g][:CHUNK] for g in range(n)]
            y = [_bf(xy[g][CHUNK:]) for g in range(n)]
        else:
            x = [x[g] + _dot(_bf(x[g]), y[g]) for g in range(n)]
    eg = [jnp.exp(col(b, h, _GAM0)) for b, h in chains]
    rhs = [_bf(jnp.concatenate([col(b, h, _BETA0) * v_ref[b, :, lanes(h)],
                                (col(b, h, _BETA0) * eg[g]) * k_ref[b, :, lanes(h)]], axis=-1))
           for g, (b, h) in enumerate(chains)]
    sol = [_dot(_bf(m), r) for m, r in zip(x, rhs)]
    ws = [_dot(jnp.concatenate([_bf(sol[g][:, A_DV:]), _bf(q_ref[b, :, lanes(h)] * eg[g])], axis=0),
               _bf(s_ref[b, h])) for g, (b, h) in enumerate(chains)]
    u = [_bf(sol[g][:, :A_DV] - ws[g][:CHUNK]) for g in range(n)]
    for g, (b, h) in enumerate(chains):
        o_ref[b, :, lanes(h)] = ws[g][CHUNK:] + _dot(_bf(qkk[g][:CHUNK] * dec[g]), u[g])
    for g, (b, h) in enumerate(chains):
        glast = auxt_ref[b, 0, _GAM0 + h:_GAM0 + h + 1, CHUNK - 1:CHUNK]
        kd = _bf(k_ref[b, :, lanes(h)] * jnp.exp(glast - col(b, h, _GAM0)))
        s_ref[b, h] = jnp.exp(glast) * s_ref[b, h] + _dot_tn(kd, u[g])

    @pl.when(c == pl.num_programs(1) - 1)
    def _():
        sfin_ref[...] = s_ref[...]


def _gdn_scan(qa, ka, va, aux, s0, *, bt):
    nb, length, _ = qa.shape
    nc = length // CHUNK
    auxt = aux[:, :, :2 * SUBLANES].reshape(nb, nc, CHUNK, 2 * SUBLANES).transpose(0, 1, 3, 2)
    row = lambda w: pl.BlockSpec((bt, CHUNK, w), lambda b, c: (b, c, 0))
    return pl.pallas_call(
        functools.partial(_gdn_kernel, bt=bt),
        grid=(nb // bt, nc),
        in_specs=[row(A_QK), row(A_QK), row(A_V), row(SMALL_W),
                  pl.BlockSpec((bt, 1, 2 * SUBLANES, CHUNK), lambda b, c: (b, c, 0, 0)),
                  pl.BlockSpec((1, HEADS, A_DK, A_DV), lambda b, c: (0, 0, 0, 0))],
        out_specs=[row(A_V), pl.BlockSpec((bt, HEADS, A_DK, A_DV), lambda b, c: (b, 0, 0, 0))],
        out_shape=[jax.ShapeDtypeStruct((nb, length, A_V), F32),
                   jax.ShapeDtypeStruct((nb, HEADS, A_DK, A_DV), F32)],
        scratch_shapes=[pltpu.VMEM((bt, HEADS, A_DK, A_DV), F32)],
        compiler_params=pltpu.CompilerParams(dimension_semantics=("arbitrary", "arbitrary")),
        name="gdn_scan",
    )(qa, ka, va, aux, auxt, s0)


def _gla_kernel(q_ref, k_ref, c_ref, v_ref, seg_ref, s0_ref, o_ref, sfin_ref, s_ref, *, bt):
    c = pl.program_id(1)

    @pl.when(c == 0)
    def _():
        for b in range(bt):
            s_ref[b] = s0_ref[0]

    row = lax.broadcasted_iota(jnp.int32, (CHUNK, LANES), 0)
    lane = lax.broadcasted_iota(jnp.int32, (CHUNK, LANES), 1)
    col = lane % B_DK
    low = lane < B_DK
    levels = (32, 16, SUBLANES)
    level_mask = [jnp.logical_and(row // (2 * s) == col // (2 * s),
                                  jnp.logical_and((row // s) % 2 == 1, (col // s) % 2 == 0)) for s in levels]
    diag_mask = jnp.logical_and(row // SUBLANES == col // SUBLANES, col % SUBLANES <= row % SUBLANES)
    zeros_v = jnp.zeros((CHUNK, B_DV), BF16)
    zeros_s = jnp.zeros((B_DK, B_DV), BF16)
    nblk = CHUNK // SUBLANES

    pairs = list(itertools.product(range(bt), range(HEADS // 2)))
    lanes_of = lambda p: slice(p * LANES, (p + 1) * LANES)
    q = [q_ref[b, :, lanes_of(p)] for b, p in pairs]
    k = [k_ref[b, :, lanes_of(p)] for b, p in pairs]
    cb = [c_ref[b, :, lanes_of(p)] for b, p in pairs]

    zs = []
    for g, (b, p) in enumerate(pairs):
        pieces = []
        for jj in range(SUBLANES):
            bcast = lambda ref: jnp.concatenate(
                [jnp.broadcast_to(ref[b, SUBLANES * i + jj:SUBLANES * i + jj + 1, lanes_of(p)], (SUBLANES, LANES))
                 for i in range(nblk)], axis=0)
            pieces.append(q[g] * bcast(k_ref) * jnp.exp2(jnp.minimum(cb[g] - bcast(c_ref), 0.0)))
        zs.append(_bf(jnp.concatenate(pieces, axis=-1)))
    diag = _dot(jnp.concatenate(zs, axis=0), seg_ref[...])

    scores = []
    for s in levels:
        lvl = []
        for g in range(len(pairs)):
            cs = cb[g].reshape(CHUNK // s, s, LANES)[:, 0:1, :]
            start = jnp.broadcast_to(cs, (CHUNK // s, s, LANES)).reshape(CHUNK, LANES)
            nxt = jnp.broadcast_to(jnp.concatenate([cs[1:], cs[-1:]], axis=0),
                                   (CHUNK // s, s, LANES)).reshape(CHUNK, LANES)
            qt = _bf(q[g] * jnp.exp2(cb[g] - start))
            kt = k[g] * jnp.exp2(jnp.minimum(nxt - cb[g], 0.0))
            kt2 = _bf(jnp.concatenate([jnp.where(low, kt, 0.0), jnp.where(low, 0.0, kt)], axis=0))
            lvl.append(_dot_nt(qt, kt2))
        scores.append(lvl)

    for g, (b, p) in enumerate(pairs):
        attn = jnp.where(diag_mask, diag[g * CHUNK:(g + 1) * CHUNK], 0.0)
        for m, lvl in zip(level_mask, scores):
            attn = jnp.where(m, lvl[g], attn)
        v0 = _bf(v_ref[b, :, (2 * p) * B_DV:(2 * p + 1) * B_DV])
        v1 = _bf(v_ref[b, :, (2 * p + 1) * B_DV:(2 * p + 2) * B_DV])
        s0 = s_ref[b, 2 * p]
        s1 = s_ref[b, 2 * p + 1]
        lhs = jnp.concatenate([_bf(q[g] * jnp.exp2(cb[g])), _bf(attn)], axis=-1)
        rhs = jnp.concatenate([jnp.concatenate([_bf(s0), zeros_s], axis=-1),
                               jnp.concatenate([zeros_s, _bf(s1)], axis=-1),
                               jnp.concatenate([v0, zeros_v], axis=-1),
                               jnp.concatenate([zeros_v, v1], axis=-1)], axis=0)
        o_ref[b, :, (2 * p) * B_DV:(2 * p + 2) * B_DV] = _dot(lhs, rhs)
        ct = cb[g].T
        clast = ct[:, CHUNK - 1:CHUNK]
        upd = _dot(_bf(k[g].T * jnp.exp2(clast - ct)), jnp.concatenate([v0, v1], axis=-1))
        decay = jnp.exp2(clast)
        s_ref[b, 2 * p] = decay[:B_DK] * s0 + upd[:B_DK, :B_DV]
        s_ref[b, 2 * p + 1] = decay[B_DK:] * s1 + upd[B_DK:, B_DV:]

    @pl.when(c == pl.num_programs(1) - 1)
    def _():
        sfin_ref[...] = s_ref[...]


def _gla_scan(qb, kb, cb, vb, s0, *, bt):
    nb, length, _ = qb.shape
    nc = length // CHUNK
    rows = np.arange(SUBLANES * LANES)
    cols = np.arange(LANES)
    seg = jnp.asarray(((rows[:, None] % LANES) // B_DK == cols[None, :] // B_DK)
                      & (rows[:, None] // LANES == cols[None, :] % SUBLANES), BF16)
    row = lambda w: pl.BlockSpec((bt, CHUNK, w), lambda b, c: (b, c, 0))
    return pl.pallas_call(
        functools.partial(_gla_kernel, bt=bt),
        grid=(nb // bt, nc),
        in_specs=[row(B_QK), row(B_QK), row(B_QK), row(B_V),
                  pl.BlockSpec(seg.shape, lambda b, c: (0, 0)),
                  pl.BlockSpec((1, HEADS, B_DK, B_DV), lambda b, c: (0, 0, 0, 0))],
        out_specs=[row(B_V), pl.BlockSpec((bt, HEADS, B_DK, B_DV), lambda b, c: (b, 0, 0, 0))],
        out_shape=[jax.ShapeDtypeStruct((nb, length, B_V), F32),
                   jax.ShapeDtypeStruct((nb, HEADS, B_DK, B_DV), F32)],
        scratch_shapes=[pltpu.VMEM((bt, HEADS, B_DK, B_DV), F32)],
        compiler_params=pltpu.CompilerParams(dimension_semantics=("arbitrary", "arbitrary")),
        name="gla_scan",
    )(qb, kb, cb, vb, seg, s0)


def _decode_kernel(qa_ref, ka_ref, va_ref, aux_ref, qb_ref, kb_ref, vb_ref, lg_ref, sg_ref, sl_ref,
                   oa_ref, ob_ref, sgn_ref, sln_ref, *, bt):
    qat = qa_ref[...].T
    kat = ka_ref[...].T
    qbt = qb_ref[...].T
    kbt = kb_ref[...].T
    decay_t = jnp.exp(lg_ref[...]).T
    for b in range(bt):
        for h in range(HEADS):
            s = sg_ref[b, h]
            kcol = kat[h * A_DK:(h + 1) * A_DK, b:b + 1]
            qcol = qat[h * A_DK:(h + 1) * A_DK, b:b + 1]
            beta = aux_ref[b:b + 1, _BETA0 + h:_BETA0 + h + 1]
            eg = jnp.exp(aux_ref[b:b + 1, _G0 + h:_G0 + h + 1])
            r = jnp.sum(kcol * s, axis=0, keepdims=True)
            u = beta * (va_ref[b:b + 1, h * A_DV:(h + 1) * A_DV] - eg * r)
            sn = eg * s + kcol * u
            sgn_ref[b, h] = sn
            oa_ref[b:b + 1, h * A_DV:(h + 1) * A_DV] = jnp.sum(qcol * sn, axis=0, keepdims=True)

            s = sl_ref[b, h]
            kcol = kbt[h * B_DK:(h + 1) * B_DK, b:b + 1]
            qcol = qbt[h * B_DK:(h + 1) * B_DK, b:b + 1]
            dcol = decay_t[h * B_DK:(h + 1) * B_DK, b:b + 1]
            sn = dcol * s + kcol * vb_ref[b:b + 1, h * B_DV:(h + 1) * B_DV]
            sln_ref[b, h] = sn
            ob_ref[b:b + 1, h * B_DV:(h + 1) * B_DV] = jnp.sum(qcol * sn, axis=0, keepdims=True)


def _decode_mix(qa, ka, va, aux, qb, kb, vb, lg, state_gdn, state_gla, l, *, bt=8):
    n = qa.shape[0]
    row = lambda w: pl.BlockSpec((bt, w), lambda i: (i, 0))
    sg_in = pl.BlockSpec((None, bt, HEADS, A_DK, A_DV), lambda i: (l, i, 0, 0, 0))
    sl_in = pl.BlockSpec((None, bt, HEADS, B_DK, B_DV), lambda i: (l, i, 0, 0, 0))
    sgs = pl.BlockSpec((bt, HEADS, A_DK, A_DV), lambda i: (i, 0, 0, 0))
    sls = pl.BlockSpec((bt, HEADS, B_DK, B_DV), lambda i: (i, 0, 0, 0))
    return pl.pallas_call(
        functools.partial(_decode_kernel, bt=bt),
        grid=(n // bt,),
        in_specs=[row(A_QK), row(A_QK), row(A_V), row(SMALL_W), row(B_QK), row(B_QK), row(B_V), row(B_QK),
                  sg_in, sl_in],
        out_specs=[row(A_V), row(B_V), sgs, sls],
        out_shape=[jax.ShapeDtypeStruct((n, A_V), F32), jax.ShapeDtypeStruct((n, B_V), F32),
                   jax.ShapeDtypeStruct(state_gdn.shape[1:], F32), jax.ShapeDtypeStruct(state_gla.shape[1:], F32)],
        compiler_params=pltpu.CompilerParams(dimension_semantics=("arbitrary",)),
        name="decode_mix",
    )(qa, ka, va, aux, qb, kb, vb, lg, state_gdn, state_gla)


def _head_rms(o, dv):
    parts = []
    for h in range(HEADS):
        oh = o[:, h * dv:(h + 1) * dv]
        parts.append(oh * lax.rsqrt(jnp.mean(oh * oh, axis=-1, keepdims=True) + 1e-6))
    return jnp.concatenate(parts, axis=-1)


def _mix_kernel(x_ref, oa_ref, ob_ref, w_ref, na_ref, nb_ref, wa_ref, wb_ref, wo_ref, g_ref, b_ref, out_ref,
                wg_ref, *, alpha):
    @pl.when(pl.program_id(0) == 0)
    def _():
        wg_ref[:, :A_V] = w_ref[:, _O_Z:_O_BETA]
        wg_ref[:, A_V:A_V + B_V] = w_ref[:, _O_RB:_O_GLR]
        wg_ref[:, A_V + B_V:] = w_ref[:, _O_GA:D_IN]

    x = x_ref[...]
    gates = _dot(_bf(x), wg_ref[...])
    za = gates[:, :A_V]
    rb = gates[:, A_V:A_V + B_V]
    ga = gates[:, A_V + B_V:A_V + B_V + D_MODEL]
    gb = gates[:, A_V + B_V + D_MODEL:]
    oa = _head_rms(oa_ref[...], A_DV) * na_ref[...] * _silu(za)
    ob = _head_rms(ob_ref[...], B_DV) * nb_ref[...] * _silu(rb)
    ya = _dot(_bf(oa), wa_ref[...])
    yb = _dot(_bf(ob), wb_ref[...])
    merged = _sigmoid(ga) * ya + _sigmoid(gb) * yb
    mix = _dot(_bf(merged), wo_ref[...])
    out_ref[...] = _layer_norm(alpha * x + mix, g_ref[...], b_ref[...])


def _mix(x, oa, ob, W, l, *, tm, alpha):
    n = x.shape[0]
    row = lambda w: pl.BlockSpec((tm, w), lambda i: (i, 0))
    names = ("w_in", "na", "nb", "wa", "wb", "wo", "ln1g", "ln1b")
    return pl.pallas_call(
        functools.partial(_mix_kernel, alpha=alpha),
        grid=(n // tm,),
        in_specs=[row(D_MODEL), row(A_V), row(B_V)] + [_layer_spec(W[k], l, W[k].dtype == BF16) for k in names],
        out_specs=row(D_MODEL),
        out_shape=jax.ShapeDtypeStruct((n, D_MODEL), F32),
        scratch_shapes=[pltpu.VMEM((D_MODEL, GATE_W), BF16)],
        compiler_params=pltpu.CompilerParams(dimension_semantics=("arbitrary",), vmem_limit_bytes=VMEM_LIMIT),
        name="mix",
    )(x, oa, ob, *[W[k] for k in names])


def _ffn_kernel(x_ref, wi_ref, wo_ref, g_ref, b_ref, out_ref, *, alpha, d_ff, fc):
    x = x_ref[...]
    xb = _bf(x)
    acc = jnp.zeros(x.shape, F32)
    for c in range(d_ff // fc):
        a = _dot(xb, wi_ref[:, c * fc:(c + 1) * fc])
        u = _dot(xb, wi_ref[:, d_ff + c * fc:d_ff + (c + 1) * fc])
        acc = acc + _dot(_bf(_silu(a) * u), wo_ref[c * fc:(c + 1) * fc, :])
    out_ref[...] = _layer_norm(alpha * x + acc, g_ref[...], b_ref[...])


def _ffn(x, W, l, *, tm, alpha):
    n = x.shape[0]
    d_ff = W["wfo"].shape[1]
    row = pl.BlockSpec((tm, D_MODEL), lambda i: (i, 0))
    names = ("wfi", "wfo", "ln2g", "ln2b")
    return pl.pallas_call(
        functools.partial(_ffn_kernel, alpha=alpha, d_ff=d_ff, fc=256),
        grid=(n // tm,),
        in_specs=[row] + [_layer_spec(W[k], l, W[k].dtype == BF16) for k in names],
        out_specs=row,
        out_shape=jax.ShapeDtypeStruct((n, D_MODEL), F32),
        compiler_params=pltpu.CompilerParams(dimension_semantics=("arbitrary",), vmem_limit_bytes=VMEM_LIMIT),
        name="ffn",
    )(x, *[W[k] for k in names])


def _stacked_params(w_in, conv_w, a_log, dt_bias, gdn_norm_w, gla_gate_w2, gla_gate_b, gla_norm_w,
                    w_branch_a, w_branch_b, w_out, ln1_g, ln1_b, ln2_g, ln2_b, w_ffn_in, w_ffn_out):
    lane_row = lambda v: jnp.pad(jnp.concatenate([v, v], axis=-1), ((0, 0), (_G0, SMALL_W - _GLR0)))
    hp = jnp.pad(jnp.stack([lane_row(a_log), lane_row(dt_bias)], axis=1), ((0, 0), (0, SUBLANES - 2), (0, 0)))
    w2 = jnp.pad(gla_gate_w2, ((0, 0), (_GLR0, SMALL_W - _GLR0 - B_GATE_RANK), (0, 0)))
    return {
        "w_in": _bf(w_in), "conv_w": conv_w, "hp": hp, "w2": _bf(w2), "gb": gla_gate_b[:, None, :],
        "na": jnp.tile(gdn_norm_w, (1, HEADS))[:, None, :], "nb": jnp.tile(gla_norm_w, (1, HEADS))[:, None, :],
        "wa": _bf(w_branch_a), "wb": _bf(w_branch_b), "wo": _bf(w_out),
        "ln1g": ln1_g[:, None, :], "ln1b": ln1_b[:, None, :], "ln2g": ln2_g[:, None, :], "ln2b": ln2_b[:, None, :],
        "wfi": _bf(w_ffn_in), "wfo": _bf(w_ffn_out),
    }


def kernel(x_prompt, x_sample, state_gdn, state_gla, state_conv, meta_tokens, w_in, conv_w, a_log, dt_bias,
           gdn_norm_w, gla_gate_w2, gla_gate_b, gla_norm_w, w_branch_a, w_branch_b, w_out,
           ln1_g, ln1_b, ln2_g, ln2_b, w_ffn_in, w_ffn_out):
    depth = w_in.shape[0]
    alpha = (2.0 * depth) ** 0.25
    nb, seq, _ = x_prompt.shape
    assert seq % CHUNK == 0 and x_sample.shape[1] == 1 and meta_tokens.shape[0] == N_META
    assert w_in.shape[2] == D_IN
    tm_proj = 256
    tm_mix = 256
    tm_ffn = 512
    bt_scan = nb
    null_rows = CHUNK - N_META
    W = _stacked_params(w_in, conv_w, a_log, dt_bias, gdn_norm_w, gla_gate_w2, gla_gate_b, gla_norm_w,
                        w_branch_a, w_branch_b, w_out, ln1_g, ln1_b, ln2_g, ln2_b, w_ffn_in, w_ffn_out)

    xp = x_prompt
    xs = x_sample[:, 0, :]
    ns = xs.shape[0]
    xm = meta_tokens
    zeros_rows = jnp.zeros((null_rows, D_MODEL), F32)
    zero_tail = jnp.zeros((SUBLANES, CONV_CH), F32)
    zero_gdn = jnp.zeros((1, HEADS, A_DK, A_DV), F32)
    zero_gla = jnp.zeros((1, HEADS, B_DK, B_DV), F32)
    outs = {k: [] for k in ("gdn_p", "gla_p", "conv_p", "gdn_s", "gla_s", "conv_s")}
    for l in range(depth):
        xm_pad = jnp.concatenate([zeros_rows, xm], axis=0)[None]
        qa, ka, va, qb, kb, vb, cb, aux, tail_m = _proj_prompt(xm_pad, zero_tail, W, l, tm=CHUNK, n_null=null_rows)
        oa_m, sg_m = _gdn_scan(qa, ka, va, aux, zero_gdn, bt=1)
        ob_m, sl_m = _gla_scan(qb, kb, cb, vb, zero_gla, bt=1)

        qa, ka, va, qb, kb, vb, cb, aux, tail_p = _proj_prompt(xp, tail_m[0], W, l, tm=tm_proj, n_null=0)
        oa_p, sg_p = _gdn_scan(qa, ka, va, aux, sg_m, bt=bt_scan)
        ob_p, sl_p = _gla_scan(qb, kb, cb, vb, sl_m, bt=bt_scan)
        x1 = _mix(xp.reshape(nb * seq, D_MODEL), oa_p.reshape(nb * seq, A_V), ob_p.reshape(nb * seq, B_V), W, l,
                  tm=tm_mix, alpha=alpha)
        xp = _ffn(x1, W, l, tm=tm_ffn, alpha=alpha).reshape(nb, seq, D_MODEL)
        outs["gdn_p"].append(sg_p)
        outs["gla_p"].append(sl_p)
        outs["conv_p"].append(tail_p[:, SUBLANES - (CONV_W - 1):, :])

        if l + 1 < depth:
            x1m = _mix(xm, oa_m[0, null_rows:], ob_m[0, null_rows:], W, l, tm=N_META, alpha=alpha)
            xm = _ffn(x1m, W, l, tm=N_META, alpha=alpha)

        qa, ka, va, qb, kb, vb, lg, aux, conv_s = _proj_decode(xs, state_conv, W, l)
        oa_s, ob_s, sg_s, sl_s = _decode_mix(qa, ka, va, aux, qb, kb, vb, lg, state_gdn, state_gla, l)
        xs = _ffn(_mix(xs, oa_s, ob_s, W, l, tm=ns, alpha=alpha), W, l, tm=ns, alpha=alpha)
        outs["gdn_s"].append(sg_s)
        outs["gla_s"].append(sl_s)
        outs["conv_s"].append(conv_s)

    return (xp, xs[:, None, :], jnp.stack(outs["gdn_p"]), jnp.stack(outs["gla_p"]), jnp.stack(outs["conv_p"]),
            jnp.stack(outs["gdn_s"]), jnp.stack(outs["gla_s"]), jnp.stack(outs["conv_s"]))
```

```python
import functools
import itertools

import numpy as np
import jax
import jax.numpy as jnp
from jax import lax
from jax.experimental import pallas as pl
from jax.experimental.pallas import tpu as pltpu

F32 = jnp.float32
BF16 = jnp.bfloat16

D_MODEL = 1024
N_META = 16
CHUNK = 64
HEADS = 4
A_DK = 128
A_DV = 128
A_QK = HEADS * A_DK
A_V = HEADS * A_DV
CONV_W = 4
CONV_CH = 2 * A_QK + A_V
B_DK = 64
B_DV = 128
B_QK = HEADS * B_DK
B_V = HEADS * B_DV
B_GATE_RANK = 16
B_GATE_TAU = 16.0
LOG2E = 1.4426950408889634
LANES = 128
SUBLANES = 8
SMALL_W = LANES
B_QKV = 2 * B_QK + B_V
SCAN_W = CONV_CH + B_QKV + SMALL_W
GATE_W = A_V + B_V + 2 * D_MODEL
VMEM_LIMIT = 56 * 1024 * 1024

_IN_SIZES = (CONV_CH, A_V, HEADS, HEADS, B_QK, B_QK, B_V, B_V, B_GATE_RANK, D_MODEL, D_MODEL)
(_O_QKV, _O_Z, _O_BETA, _O_A, _O_QB, _O_KB, _O_VB, _O_RB, _O_GLR, _O_GA, _O_GB, D_IN) = (
    int(v) for v in np.concatenate([[0], np.cumsum(_IN_SIZES)]))

_BETA0, _G0, _GAM0, _GLR0 = 0, 4, 8, 12


def _dot(a, b):
    return jnp.dot(a, b, preferred_element_type=F32)


def _dot_nt(a, b):
    return lax.dot_general(a, b, (((1,), (1,)), ((), ())), preferred_element_type=F32)


def _dot_tn(a, b):
    return lax.dot_general(a, b, (((0,), (0,)), ((), ())), preferred_element_type=F32)


def _bf(x):
    return x.astype(BF16)


def _tri_cumsum(ltri, x):
    hi = x.astype(BF16)
    r1 = x - hi.astype(F32)
    mid = r1.astype(BF16)
    lo = (r1 - mid.astype(F32)).astype(BF16)
    return _dot(ltri, hi) + _dot(ltri, mid) + _dot(ltri, lo)


def _sigmoid(x):
    return 1.0 / (1.0 + jnp.exp(-x))


def _silu(x):
    return x * _sigmoid(x)


def _softplus(x):
    return jnp.maximum(x, 0.0) + jnp.log1p(jnp.exp(-jnp.abs(x)))


def _log_sigmoid(x):
    return jnp.minimum(x, 0.0) - jnp.log1p(jnp.exp(-jnp.abs(x)))


def _layer_norm(h, g, b):
    mu = jnp.mean(h, axis=-1, keepdims=True)
    d = h - mu
    var = jnp.mean(d * d, axis=-1, keepdims=True)
    return d * lax.rsqrt(var + 1e-5) * g + b


def _layer_spec(arr, l, single_buffer=False):
    kw = {"pipeline_mode": pl.Buffered(1)} if single_buffer else {}
    zeros = (0,) * (arr.ndim - 1)
    return pl.BlockSpec((None,) + arr.shape[1:], lambda *_: (l,) + zeros, **kw)


def _regroup_scan_weights(w_ref, wsc_ref):
    wsc_ref[:, :CONV_CH] = w_ref[:, _O_QKV:_O_Z]
    wsc_ref[:, CONV_CH:CONV_CH + B_QKV] = w_ref[:, _O_QB:_O_RB]
    s0 = CONV_CH + B_QKV
    wsc_ref[:, s0:s0 + SMALL_W] = jnp.zeros((D_MODEL, SMALL_W), BF16)
    wsc_ref[:, s0 + _BETA0:s0 + _GAM0] = w_ref[:, _O_BETA:_O_QB]
    wsc_ref[:, s0 + _GAM0:s0 + _GLR0] = w_ref[:, _O_A:_O_QB]
    wsc_ref[:, s0 + _GLR0:s0 + _GLR0 + B_GATE_RANK] = w_ref[:, _O_GLR:_O_GA]


def _proj_tail(p, act, hp, w2, gb):
    qs, ks = [], []
    for h in range(HEADS):
        qh = act[:, h * A_DK:(h + 1) * A_DK]
        qs.append(qh * lax.rsqrt(jnp.sum(qh * qh, axis=-1, keepdims=True) + 1e-6) * (A_DK ** -0.5))
        kh = act[:, A_QK + h * A_DK:A_QK + (h + 1) * A_DK]
        ks.append(kh * lax.rsqrt(jnp.sum(kh * kh, axis=-1, keepdims=True) + 1e-6))
    qa = jnp.concatenate(qs, axis=-1)
    ka = jnp.concatenate(ks, axis=-1)
    va = act[:, 2 * A_QK:]
    qb = p[:, CONV_CH:CONV_CH + B_QK] * (B_DK ** -0.5)
    kb = p[:, CONV_CH + B_QK:CONV_CH + 2 * B_QK]
    vb = p[:, CONV_CH + 2 * B_QK:CONV_CH + B_QKV]
    small = p[:, CONV_CH + B_QKV:]
    beta = _sigmoid(small)
    g = -(jnp.exp(hp[0:1, :]) * _softplus(small + hp[1:2, :]))
    lg = _log_sigmoid(_dot(_bf(small), w2) + gb) * (1.0 / B_GATE_TAU)
    return qa, ka, va, qb, kb, vb, beta, g, lg


def _proj_prompt_kernel(x_ref, carry0_ref, w_ref, cw_ref, hp_ref, w2_ref, gb_ref, ltri_ref,
                        qa_ref, ka_ref, va_ref, qb_ref, kb_ref, vb_ref, cb_ref, aux_ref, tail_ref,
                        wsc_ref, carry_ref, *, tm, nsub, n_null):
    @pl.when(jnp.logical_and(pl.program_id(0) == 0, pl.program_id(1) == 0))
    def _():
        _regroup_scan_weights(w_ref, wsc_ref)

    @pl.when(pl.program_id(1) == 0)
    def _():
        carry_ref[...] = carry0_ref[...]

    ps = [_dot(_bf(x_ref[0, i * tm:(i + 1) * tm, :]), wsc_ref[...]) for i in range(nsub)]
    prev8 = carry_ref[...]
    cw = cw_ref[...]
    row8 = lax.broadcasted_iota(jnp.int32, (SUBLANES, CONV_CH), 0)
    lane = lax.broadcasted_iota(jnp.int32, (tm, SMALL_W), 1)
    ltri = ltri_ref[...]
    for i, p in enumerate(ps):
        rows = slice(i * tm, (i + 1) * tm)
        u = p[:, :CONV_CH]
        acc = u * cw[CONV_W - 1:CONV_W, :]
        for s in range(1, CONV_W):
            rolled = pltpu.roll(u, s, 0)
            head = jnp.where(row8 < s, pltpu.roll(prev8, s, 0), rolled[:SUBLANES])
            shifted = jnp.concatenate([head, rolled[SUBLANES:]], axis=0)
            acc = acc + shifted * cw[CONV_W - 1 - s:CONV_W - s, :]
        prev8 = u[tm - SUBLANES:, :]

        qa, ka, va, qb, kb, vb, beta, g, lg = _proj_tail(p, _silu(acc), hp_ref[...], w2_ref[...], gb_ref[...])
        if n_null:
            live = lax.broadcasted_iota(jnp.int32, (tm, 1), 0) >= n_null - i * tm
            g = jnp.where(live, g, 0.0)
            lg = jnp.where(live, lg, 0.0)
        gam = _tri_cumsum(ltri, g)
        cb = _tri_cumsum(ltri, lg * LOG2E)
        aux = jnp.where(lane < _G0, beta, jnp.where(lane < _GAM0, g, jnp.where(lane < _GLR0, gam, 0.0)))
        qa_ref[0, rows, :] = qa
        ka_ref[0, rows, :] = ka
        va_ref[0, rows, :] = va
        qb_ref[0, rows, :] = qb
        kb_ref[0, rows, :] = kb
        vb_ref[0, rows, :] = vb
        cb_ref[0, rows, :] = cb
        aux_ref[0, rows, :] = aux
    carry_ref[...] = prev8
    tail_ref[0] = prev8


def _proj_prompt(x, carry0, W, l, *, tm, nsub, n_null):
    nb, length, _ = x.shape
    blk = np.arange(tm) // CHUNK
    ltri = jnp.asarray((blk[:, None] == blk[None, :]) & (np.arange(tm)[:, None] >= np.arange(tm)[None, :]), BF16)
    full = lambda shape: pl.BlockSpec(shape, lambda b, j: (0,) * len(shape))
    row = lambda w: pl.BlockSpec((1, tm * nsub, w), lambda b, j: (b, j, 0))
    out_w = (A_QK, A_QK, A_V, B_QK, B_QK, B_V, B_QK, SMALL_W)
    return pl.pallas_call(
        functools.partial(_proj_prompt_kernel, tm=tm, nsub=nsub, n_null=n_null),
        grid=(nb, length // (tm * nsub)),
        in_specs=[row(D_MODEL), full((SUBLANES, CONV_CH)), _layer_spec(W["w_in"], l, True),
                  _layer_spec(W["conv_w"], l), _layer_spec(W["hp"], l), _layer_spec(W["w2"], l),
                  _layer_spec(W["gb"], l), full((tm, tm))],
        out_specs=[row(w) for w in out_w] + [pl.BlockSpec((1, SUBLANES, CONV_CH), lambda b, j: (b, 0, 0))],
        out_shape=[jax.ShapeDtypeStruct((nb, length, w), F32) for w in out_w]
        + [jax.ShapeDtypeStruct((nb, SUBLANES, CONV_CH), F32)],
        scratch_shapes=[pltpu.VMEM((D_MODEL, SCAN_W), BF16), pltpu.VMEM((SUBLANES, CONV_CH), F32)],
        compiler_params=pltpu.CompilerParams(dimension_semantics=("arbitrary", "arbitrary"),
                                             vmem_limit_bytes=VMEM_LIMIT),
        name="proj_prompt",
    )(x, carry0, W["w_in"], W["conv_w"], W["hp"], W["w2"], W["gb"], ltri)


def _proj_decode_kernel(x_ref, st_ref, w_ref, cw_ref, hp_ref, w2_ref, gb_ref,
                        qa_ref, ka_ref, va_ref, qb_ref, kb_ref, vb_ref, lg_ref, aux_ref, stn_ref, wsc_ref):
    _regroup_scan_weights(w_ref, wsc_ref)
    p = _dot(_bf(x_ref[...]), wsc_ref[...])
    u = p[:, :CONV_CH]
    cw = cw_ref[...]
    st1 = st_ref[:, 1, :]
    st2 = st_ref[:, 2, :]
    acc = st_ref[:, 0, :] * cw[0:1, :] + st1 * cw[1:2, :] + st2 * cw[2:3, :] + u * cw[3:4, :]
    qa, ka, va, qb, kb, vb, beta, g, lg = _proj_tail(p, _silu(acc), hp_ref[...], w2_ref[...], gb_ref[...])
    lane = lax.broadcasted_iota(jnp.int32, beta.shape, 1)
    qa_ref[...] = qa
    ka_ref[...] = ka
    va_ref[...] = va
    qb_ref[...] = qb
    kb_ref[...] = kb
    vb_ref[...] = vb
    lg_ref[...] = lg
    aux_ref[...] = jnp.where(lane < _G0, beta, jnp.where(lane < _GAM0, g, 0.0))
    stn_ref[:, 0, :] = st1
    stn_ref[:, 1, :] = st2
    stn_ref[:, 2, :] = u


def _proj_decode(x, state_conv, W, l):
    n = x.shape[0]
    out_w = (A_QK, A_QK, A_V, B_QK, B_QK, B_V, B_QK, SMALL_W)
    whole = lambda shape: pl.BlockSpec(shape, lambda i: (0,) * len(shape))
    return pl.pallas_call(
        _proj_decode_kernel,
        grid=(1,),
        in_specs=[whole((n, D_MODEL)), _layer_spec(state_conv, l), _layer_spec(W["w_in"], l, True),
                  _layer_spec(W["conv_w"], l), _layer_spec(W["hp"], l), _layer_spec(W["w2"], l),
                  _layer_spec(W["gb"], l)],
        out_specs=[whole((n, w)) for w in out_w] + [whole((n, CONV_W - 1, CONV_CH))],
        out_shape=[jax.ShapeDtypeStruct((n, w), F32) for w in out_w]
        + [jax.ShapeDtypeStruct((n, CONV_W - 1, CONV_CH), F32)],
        scratch_shapes=[pltpu.VMEM((D_MODEL, SCAN_W), BF16)],
        compiler_params=pltpu.CompilerParams(dimension_semantics=("arbitrary",), vmem_limit_bytes=VMEM_LIMIT),
        name="proj_decode",
    )(x, state_conv, W["w_in"], W["conv_w"], W["hp"], W["w2"], W["gb"])


def _gdn_kernel(q_ref, k_ref, v_ref, aux_ref, auxt_ref, s0_ref, o_ref, sfin_ref, s_ref, *, bt):
    c = pl.program_id(1)

    @pl.when(c == 0)
    def _():
        for b in range(bt):
            s_ref[b] = s0_ref[0]

    ri = lax.broadcasted_iota(jnp.int32, (CHUNK, CHUNK), 0)
    ci = lax.broadcasted_iota(jnp.int32, (CHUNK, CHUNK), 1)
    eye = (ri == ci).astype(F32)
    chains = list(itertools.product(range(bt), range(HEADS)))
    n = len(chains)
    lanes = lambda h: slice(h * A_DK, (h + 1) * A_DK)
    col = lambda b, h, off: aux_ref[b, :, off + h:off + h + 1]
    kb = [_bf(k_ref[b, :, lanes(h)]) for b, h in chains]
    qkk = [_dot_nt(jnp.concatenate([_bf(q_ref[b, :, lanes(h)]), kb[g]], axis=0), kb[g])
           for g, (b, h) in enumerate(chains)]
    dec = [jnp.exp(jnp.where(ri >= ci, col(b, h, _GAM0) - auxt_ref[b, 0, _GAM0 + h:_GAM0 + h + 1, :], -jnp.inf))
           for b, h in chains]
    a = [jnp.where(ri > ci, col(b, h, _BETA0) * dec[g] * qkk[g][CHUNK:], 0.0) for g, (b, h) in enumerate(chains)]
    x = [eye - m for m in a]
    ab = [_bf(m) for m in a]
    y = [_bf(_dot(m, m)) for m in ab]
    steps = int(np.log2(CHUNK)) - 1
    for i in range(steps):
        if i + 1 < steps:
            xy = [_dot(jnp.concatenate([_bf(x[g]), y[g]], axis=0), y[g]) for g in range(n)]
            x = [x[g] + xy[g][:CHUNK] for g in range(n)]
            y = [_bf(xy[g][CHUNK:]) for g in range(n)]
        else:
            x = [x[g] + _dot(_bf(x[g]), y[g]) for g in range(n)]
    eg = [jnp.exp(col(b, h, _GAM0)) for b, h in chains]
    rhs = [_bf(jnp.concatenate([col(b, h, _BETA0) * v_ref[b, :, lanes(h)],
                                (col(b, h, _BETA0) * eg[g]) * k_ref[b, :, lanes(h)]], axis=-1))
           for g, (b, h) in enumerate(chains)]
    sol = [_dot(_bf(m), r) for m, r in zip(x, rhs)]
    ws = [_dot(jnp.concatenate([_bf(sol[g][:, A_DV:]), _bf(q_ref[b, :, lanes(h)] * eg[g])], axis=0),
               _bf(s_ref[b, h])) for g, (b, h) in enumerate(chains)]
    u = [_bf(sol[g][:, :A_DV] - ws[g][:CHUNK]) for g in range(n)]
    for g, (b, h) in enumerate(chains):
        o_ref[b, :, lanes(h)] = ws[g][CHUNK:] + _dot(_bf(qkk[g][:CHUNK] * dec[g]), u[g])
    for g, (b, h) in enumerate(chains):
        glast = auxt_ref[b, 0, _GAM0 + h:_GAM0 + h + 1, CHUNK - 1:CHUNK]
        kd = _bf(k_ref[b, :, lanes(h)] * jnp.exp(glast - col(b, h, _GAM0)))
        s_ref[b, h] = jnp.exp(glast) * s_ref[b, h] + _dot_tn(kd, u[g])

    @pl.when(c == pl.num_programs(1) - 1)
    def _():
        sfin_ref[...] = s_ref[...]


def _gdn_scan(qa, ka, va, aux, s0, *, bt):
    nb, length, _ = qa.shape
    nc = length // CHUNK
    auxt = aux[:, :, :2 * SUBLANES].reshape(nb, nc, CHUNK, 2 * SUBLANES).transpose(0, 1, 3, 2)
    row = lambda w: pl.BlockSpec((bt, CHUNK, w), lambda b, c: (b, c, 0))
    return pl.pallas_call(
        functools.partial(_gdn_kernel, bt=bt),
        grid=(nb // bt, nc),
        in_specs=[row(A_QK), row(A_QK), row(A_V), row(SMALL_W),
                  pl.BlockSpec((bt, 1, 2 * SUBLANES, CHUNK), lambda b, c: (b, c, 0, 0)),
                  pl.BlockSpec((1, HEADS, A_DK, A_DV), lambda b, c: (0, 0, 0, 0))],
        out_specs=[row(A_V), pl.BlockSpec((bt, HEADS, A_DK, A_DV), lambda b, c: (b, 0, 0, 0))],
        out_shape=[jax.ShapeDtypeStruct((nb, length, A_V), F32),
                   jax.ShapeDtypeStruct((nb, HEADS, A_DK, A_DV), F32)],
        scratch_shapes=[pltpu.VMEM((bt, HEADS, A_DK, A_DV), F32)],
        compiler_params=pltpu.CompilerParams(dimension_semantics=("arbitrary", "arbitrary")),
        name="gdn_scan",
    )(qa, ka, va, aux, auxt, s0)


def _gla_kernel(q_ref, k_ref, c_ref, v_ref, seg_ref, s0_ref, o_ref, sfin_ref, s_ref, *, bt):
    c = pl.program_id(1)

    @pl.when(c == 0)
    def _():
        for b in range(bt):
            s_ref[b] = s0_ref[0]

    row = lax.broadcasted_iota(jnp.int32, (CHUNK, LANES), 0)
    lane = lax.broadcasted_iota(jnp.int32, (CHUNK, LANES), 1)
    col = lane % B_DK
    low = lane < B_DK
    levels = (32, 16, SUBLANES)
    level_mask = [jnp.logical_and(row // (2 * s) == col // (2 * s),
                                  jnp.logical_and((row // s) % 2 == 1, (col // s) % 2 == 0)) for s in levels]
    diag_mask = jnp.logical_and(row // SUBLANES == col // SUBLANES, col % SUBLANES <= row % SUBLANES)
    zeros_v = jnp.zeros((CHUNK, B_DV), BF16)
    zeros_s = jnp.zeros((B_DK, B_DV), BF16)
    nblk = CHUNK // SUBLANES

    pairs = list(itertools.product(range(bt), range(HEADS // 2)))
    lanes_of = lambda p: slice(p * LANES, (p + 1) * LANES)
    q = [q_ref[b, :, lanes_of(p)] for b, p in pairs]
    k = [k_ref[b, :, lanes_of(p)] for b, p in pairs]
    cb = [c_ref[b, :, lanes_of(p)] for b, p in pairs]

    zs = []
    for g, (b, p) in enumerate(pairs):
        pieces = []
        for jj in range(SUBLANES):
            bcast = lambda ref: jnp.concatenate(
                [jnp.broadcast_to(ref[b, SUBLANES * i + jj:SUBLANES * i + jj + 1, lanes_of(p)], (SUBLANES, LANES))
                 for i in range(nblk)], axis=0)
            pieces.append(q[g] * bcast(k_ref) * jnp.exp2(jnp.minimum(cb[g] - bcast(c_ref), 0.0)))
        zs.append(_bf(jnp.concatenate(pieces, axis=-1)))
    diag = _dot(jnp.concatenate(zs, axis=0), seg_ref[...])

    scores = []
    for s in levels:
        lvl = []
        for g in range(len(pairs)):
            cs = cb[g].reshape(CHUNK // s, s, LANES)[:, 0:1, :]
            start = jnp.broadcast_to(cs, (CHUNK // s, s, LANES)).reshape(CHUNK, LANES)
            nxt = jnp.broadcast_to(jnp.concatenate([cs[1:], cs[-1:]], axis=0),
                                   (CHUNK // s, s, LANES)).reshape(CHUNK, LANES)
            qt = _bf(q[g] * jnp.exp2(cb[g] - start))
            kt = k[g] * jnp.exp2(jnp.minimum(nxt - cb[g], 0.0))
            kt2 = _bf(jnp.concatenate([jnp.where(low, kt, 0.0), jnp.where(low, 0.0, kt)], axis=0))
            lvl.append(_dot_nt(qt, kt2))
        scores.append(lvl)

    for g, (b, p) in enumerate(pairs):
        attn = jnp.where(diag_mask, diag[g * CHUNK:(g + 1) * CHUNK], 0.0)
        for m, lvl in zip(level_mask, scores):
            attn = jnp.where(m, lvl[g], attn)
        v0 = _bf(v_ref[b, :, (2 * p) * B_DV:(2 * p + 1) * B_DV])
        v1 = _bf(v_ref[b, :, (2 * p + 1) * B_DV:(2 * p + 2) * B_DV])
        s0 = s_ref[b, 2 * p]
        s1 = s_ref[b, 2 * p + 1]
        lhs = jnp.concatenate([_bf(q[g] * jnp.exp2(cb[g])), _bf(attn)], axis=-1)
        rhs = jnp.concatenate([jnp.concatenate([_bf(s0), zeros_s], axis=-1),
                               jnp.concatenate([zeros_s, _bf(s1)], axis=-1),
                               jnp.concatenate([v0, zeros_v], axis=-1),
                               jnp.concatenate([zeros_v, v1], axis=-1)], axis=0)
        o_ref[b, :, (2 * p) * B_DV:(2 * p + 2) * B_DV] = _dot(lhs, rhs)
        ct = cb[g].T
        clast = ct[:, CHUNK - 1:CHUNK]
        upd = _dot(_bf(k[g].T * jnp.exp2(clast - ct)), jnp.concatenate([v0, v1], axis=-1))
        decay = jnp.exp2(clast)
        s_ref[b, 2 * p] = decay[:B_DK] * s0 + upd[:B_DK, :B_DV]
        s_ref[b, 2 * p + 1] = decay[B_DK:] * s1 + upd[B_DK:, B_DV:]

    @pl.when(c == pl.num_programs(1) - 1)
    def _():
        sfin_ref[...] = s_ref[...]


def _gla_scan(qb, kb, cb, vb, s0, *, bt):
    nb, length, _ = qb.shape
    nc = length // CHUNK
    rows = np.arange(SUBLANES * LANES)
    cols = np.arange(LANES)
    seg = jnp.asarray(((rows[:, None] % LANES) // B_DK == cols[None, :] // B_DK)
                      & (rows[:, None] // LANES == cols[None, :] % SUBLANES), BF16)
    row = lambda w: pl.BlockSpec((bt, CHUNK, w), lambda b, c: (b, c, 0))
    return pl.pallas_call(
        functools.partial(_gla_kernel, bt=bt),
        grid=(nb // bt, nc),
        in_specs=[row(B_QK), row(B_QK), row(B_QK), row(B_V),
                  pl.BlockSpec(seg.shape, lambda b, c: (0, 0)),
                  pl.BlockSpec((1, HEADS, B_DK, B_DV), lambda b, c: (0, 0, 0, 0))],
        out_specs=[row(B_V), pl.BlockSpec((bt, HEADS, B_DK, B_DV), lambda b, c: (b, 0, 0, 0))],
        out_shape=[jax.ShapeDtypeStruct((nb, length, B_V), F32),
                   jax.ShapeDtypeStruct((nb, HEADS, B_DK, B_DV), F32)],
        scratch_shapes=[pltpu.VMEM((bt, HEADS, B_DK, B_DV), F32)],
        compiler_params=pltpu.CompilerParams(dimension_semantics=("arbitrary", "arbitrary")),
        name="gla_scan",
    )(qb, kb, cb, vb, seg, s0)


def _decode_kernel(qa_ref, ka_ref, va_ref, aux_ref, qb_ref, kb_ref, vb_ref, lg_ref, sg_ref, sl_ref,
                   oa_ref, ob_ref, sgn_ref, sln_ref, *, bt):
    qat = qa_ref[...].T
    kat = ka_ref[...].T
    qbt = qb_ref[...].T
    kbt = kb_ref[...].T
    decay_t = jnp.exp(lg_ref[...]).T
    for b in range(bt):
        for h in range(HEADS):
            s = sg_ref[b, h]
            kcol = kat[h * A_DK:(h + 1) * A_DK, b:b + 1]
            qcol = qat[h * A_DK:(h + 1) * A_DK, b:b + 1]
            beta = aux_ref[b:b + 1, _BETA0 + h:_BETA0 + h + 1]
            eg = jnp.exp(aux_ref[b:b + 1, _G0 + h:_G0 + h + 1])
            r = jnp.sum(kcol * s, axis=0, keepdims=True)
            u = beta * (va_ref[b:b + 1, h * A_DV:(h + 1) * A_DV] - eg * r)
            sn = eg * s + kcol * u
            sgn_ref[b, h] = sn
            oa_ref[b:b + 1, h * A_DV:(h + 1) * A_DV] = jnp.sum(qcol * sn, axis=0, keepdims=True)

            s = sl_ref[b, h]
            kcol = kbt[h * B_DK:(h + 1) * B_DK, b:b + 1]
            qcol = qbt[h * B_DK:(h + 1) * B_DK, b:b + 1]
            dcol = decay_t[h * B_DK:(h + 1) * B_DK, b:b + 1]
            sn = dcol * s + kcol * vb_ref[b:b + 1, h * B_DV:(h + 1) * B_DV]
            sln_ref[b, h] = sn
            ob_ref[b:b + 1, h * B_DV:(h + 1) * B_DV] = jnp.sum(qcol * sn, axis=0, keepdims=True)


def _decode_mix(qa, ka, va, aux, qb, kb, vb, lg, state_gdn, state_gla, l, *, bt=8):
    n = qa.shape[0]
    row = lambda w: pl.BlockSpec((bt, w), lambda i: (i, 0))
    sg_in = pl.BlockSpec((None, bt, HEADS, A_DK, A_DV), lambda i: (l, i, 0, 0, 0))
    sl_in = pl.BlockSpec((None, bt, HEADS, B_DK, B_DV), lambda i: (l, i, 0, 0, 0))
    sgs = pl.BlockSpec((bt, HEADS, A_DK, A_DV), lambda i: (i, 0, 0, 0))
    sls = pl.BlockSpec((bt, HEADS, B_DK, B_DV), lambda i: (i, 0, 0, 0))
    return pl.pallas_call(
        functools.partial(_decode_kernel, bt=bt),
        grid=(n // bt,),
        in_specs=[row(A_QK), row(A_QK), row(A_V), row(SMALL_W), row(B_QK), row(B_QK), row(B_V), row(B_QK),
                  sg_in, sl_in],
        out_specs=[row(A_V), row(B_V), sgs, sls],
        out_shape=[jax.ShapeDtypeStruct((n, A_V), F32), jax.ShapeDtypeStruct((n, B_V), F32),
                   jax.ShapeDtypeStruct(state_gdn.shape[1:], F32), jax.ShapeDtypeStruct(state_gla.shape[1:], F32)],
        compiler_params=pltpu.CompilerParams(dimension_semantics=("arbitrary",)),
        name="decode_mix",
    )(qa, ka, va, aux, qb, kb, vb, lg, state_gdn, state_gla)


def _head_rms(o, dv):
    parts = []
    for h in range(HEADS):
        oh = o[:, h * dv:(h + 1) * dv]
        parts.append(oh * lax.rsqrt(jnp.mean(oh * oh, axis=-1, keepdims=True) + 1e-6))
    return jnp.concatenate(parts, axis=-1)


def _mix_kernel(x_ref, oa_ref, ob_ref, w_ref, na_ref, nb_ref, wa_ref, wb_ref, wo_ref, g_ref, b_ref, out_ref,
                wg_ref, *, tm, nsub, alpha):
    @pl.when(pl.program_id(0) == 0)
    def _():
        wg_ref[:, :A_V] = w_ref[:, _O_Z:_O_BETA]
        wg_ref[:, A_V:A_V + B_V] = w_ref[:, _O_RB:_O_GLR]
        wg_ref[:, A_V + B_V:] = w_ref[:, _O_GA:D_IN]

    subs = [slice(i * tm, (i + 1) * tm) for i in range(nsub)]
    gates = [_dot(_bf(x_ref[r, :]), wg_ref[...]) for r in subs]
    ys = []
    for r, gt in zip(subs, gates):
        oa = _head_rms(oa_ref[r, :], A_DV) * na_ref[...] * _silu(gt[:, :A_V])
        ob = _head_rms(ob_ref[r, :], B_DV) * nb_ref[...] * _silu(gt[:, A_V:A_V + B_V])
        ys.append((_dot(_bf(oa), wa_ref[...]), _dot(_bf(ob), wb_ref[...])))
    mixes = []
    for gt, (ya, yb) in zip(gates, ys):
        merged = (_sigmoid(gt[:, A_V + B_V:A_V + B_V + D_MODEL]) * ya
                  + _sigmoid(gt[:, A_V + B_V + D_MODEL:]) * yb)
        mixes.append(_dot(_bf(merged), wo_ref[...]))
    for r, mix in zip(subs, mixes):
        out_ref[r, :] = _layer_norm(alpha * x_ref[r, :] + mix, g_ref[...], b_ref[...])


def _mix(x, oa, ob, W, l, *, tm, nsub, alpha):
    n = x.shape[0]
    row = lambda w: pl.BlockSpec((tm * nsub, w), lambda i: (i, 0))
    names = ("w_in", "na", "nb", "wa", "wb", "wo", "ln1g", "ln1b")
    return pl.pallas_call(
        functools.partial(_mix_kernel, tm=tm, nsub=nsub, alpha=alpha),
        grid=(n // (tm * nsub),),
        in_specs=[row(D_MODEL), row(A_V), row(B_V)] + [_layer_spec(W[k], l, W[k].dtype == BF16) for k in names],
        out_specs=row(D_MODEL),
        out_shape=jax.ShapeDtypeStruct((n, D_MODEL), F32),
        scratch_shapes=[pltpu.VMEM((D_MODEL, GATE_W), BF16)],
        compiler_params=pltpu.CompilerParams(dimension_semantics=("arbitrary",), vmem_limit_bytes=VMEM_LIMIT),
        name="mix",
    )(x, oa, ob, *[W[k] for k in names])


def _ffn_kernel(x_ref, wi_ref, wo_ref, g_ref, b_ref, out_ref, *, alpha, d_ff, fc):
    x = x_ref[...]
    xb = _bf(x)
    acc = jnp.zeros(x.shape, F32)
    for c in range(d_ff // fc):
        a = _dot(xb, wi_ref[:, c * fc:(c + 1) * fc])
        u = _dot(xb, wi_ref[:, d_ff + c * fc:d_ff + (c + 1) * fc])
        acc = acc + _dot(_bf(_silu(a) * u), wo_ref[c * fc:(c + 1) * fc, :])
    out_ref[...] = _layer_norm(alpha * x + acc, g_ref[...], b_ref[...])


def _ffn(x, W, l, *, tm, alpha):
    n = x.shape[0]
    d_ff = W["wfo"].shape[1]
    row = pl.BlockSpec((tm, D_MODEL), lambda i: (i, 0))
    names = ("wfi", "wfo", "ln2g", "ln2b")
    return pl.pallas_call(
        functools.partial(_ffn_kernel, alpha=alpha, d_ff=d_ff, fc=256),
        grid=(n // tm,),
        in_specs=[row] + [_layer_spec(W[k], l, W[k].dtype == BF16) for k in names],
        out_specs=row,
        out_shape=jax.ShapeDtypeStruct((n, D_MODEL), F32),
        compiler_params=pltpu.CompilerParams(dimension_semantics=("arbitrary",), vmem_limit_bytes=VMEM_LIMIT),
        name="ffn",
    )(x, *[W[k] for k in names])


def _stacked_params(w_in, conv_w, a_log, dt_bias, gdn_norm_w, gla_gate_w2, gla_gate_b, gla_norm_w,
                    w_branch_a, w_branch_b, w_out, ln1_g, ln1_b, ln2_g, ln2_b, w_ffn_in, w_ffn_out):
    lane_row = lambda v: jnp.pad(jnp.concatenate([v, v], axis=-1), ((0, 0), (_G0, SMALL_W - _GLR0)))
    hp = jnp.pad(jnp.stack([lane_row(a_log), lane_row(dt_bias)], axis=1), ((0, 0), (0, SUBLANES - 2), (0, 0)))
    w2 = jnp.pad(gla_gate_w2, ((0, 0), (_GLR0, SMALL_W - _GLR0 - B_GATE_RANK), (0, 0)))
    return {
        "w_in": _bf(w_in), "conv_w": conv_w, "hp": hp, "w2": _bf(w2), "gb": gla_gate_b[:, None, :],
        "na": jnp.tile(gdn_norm_w, (1, HEADS))[:, None, :], "nb": jnp.tile(gla_norm_w, (1, HEADS))[:, None, :],
        "wa": _bf(w_branch_a), "wb": _bf(w_branch_b), "wo": _bf(w_out),
        "ln1g": ln1_g[:, None, :], "ln1b": ln1_b[:, None, :], "ln2g": ln2_g[:, None, :], "ln2b": ln2_b[:, None, :],
        "wfi": _bf(w_ffn_in), "wfo": _bf(w_ffn_out),
    }


def kernel(x_prompt, x_sample, state_gdn, state_gla, state_conv, meta_tokens, w_in, conv_w, a_log, dt_bias,
           gdn_norm_w, gla_gate_w2, gla_gate_b, gla_norm_w, w_branch_a, w_branch_b, w_out,
           ln1_g, ln1_b, ln2_g, ln2_b, w_ffn_in, w_ffn_out):
    depth = w_in.shape[0]
    alpha = (2.0 * depth) ** 0.25
    nb, seq, _ = x_prompt.shape
    assert seq % CHUNK == 0 and x_sample.shape[1] == 1 and meta_tokens.shape[0] == N_META
    assert w_in.shape[2] == D_IN
    tm_proj = 256
    nsub_proj = 4
    tm_mix = 256
    nsub_mix = 2
    tm_ffn = 512
    bt_scan = nb
    null_rows = CHUNK - N_META
    W = _stacked_params(w_in, conv_w, a_log, dt_bias, gdn_norm_w, gla_gate_w2, gla_gate_b, gla_norm_w,
                        w_branch_a, w_branch_b, w_out, ln1_g, ln1_b, ln2_g, ln2_b, w_ffn_in, w_ffn_out)

    xp = x_prompt
    xs = x_sample[:, 0, :]
    ns = xs.shape[0]
    xm = meta_tokens
    zeros_rows = jnp.zeros((null_rows, D_MODEL), F32)
    zero_tail = jnp.zeros((SUBLANES, CONV_CH), F32)
    zero_gdn = jnp.zeros((1, HEADS, A_DK, A_DV), F32)
    zero_gla = jnp.zeros((1, HEADS, B_DK, B_DV), F32)
    outs = {k: [] for k in ("gdn_p", "gla_p", "conv_p", "gdn_s", "gla_s", "conv_s")}
    for l in range(depth):
        xm_pad = jnp.concatenate([zeros_rows, xm], axis=0)[None]
        qa, ka, va, qb, kb, vb, cb, aux, tail_m = _proj_prompt(xm_pad, zero_tail, W, l, tm=CHUNK, nsub=1, n_null=null_rows)
        oa_m, sg_m = _gdn_scan(qa, ka, va, aux, zero_gdn, bt=1)
        ob_m, sl_m = _gla_scan(qb, kb, cb, vb, zero_gla, bt=1)

        qa, ka, va, qb, kb, vb, cb, aux, tail_p = _proj_prompt(xp, tail_m[0], W, l, tm=tm_proj, nsub=nsub_proj, n_null=0)
        oa_p, sg_p = _gdn_scan(qa, ka, va, aux, sg_m, bt=bt_scan)
        ob_p, sl_p = _gla_scan(qb, kb, cb, vb, sl_m, bt=bt_scan)
        x1 = _mix(xp.reshape(nb * seq, D_MODEL), oa_p.reshape(nb * seq, A_V), ob_p.reshape(nb * seq, B_V), W, l,
                  tm=tm_mix, nsub=nsub_mix, alpha=alpha)
        xp = _ffn(x1, W, l, tm=tm_ffn, alpha=alpha).reshape(nb, seq, D_MODEL)
        outs["gdn_p"].append(sg_p)
        outs["gla_p"].append(sl_p)
        outs["conv_p"].append(tail_p[:, SUBLANES - (CONV_W - 1):, :])

        if l + 1 < depth:
            x1m = _mix(xm, oa_m[0, null_rows:], ob_m[0, null_rows:], W, l, tm=N_META, nsub=1, alpha=alpha)
            xm = _ffn(x1m, W, l, tm=N_META, alpha=alpha)

        qa, ka, va, qb, kb, vb, lg, aux, conv_s = _proj_decode(xs, state_conv, W, l)
        oa_s, ob_s, sg_s, sl_s = _decode_mix(qa, ka, va, aux, qb, kb, vb, lg, state_gdn, state_gla, l)
        xs = _ffn(_mix(xs, oa_s, ob_s, W, l, tm=ns, nsub=1, alpha=alpha), W, l, tm=ns, alpha=alpha)
        outs["gdn_s"].append(sg_s)
        outs["gla_s"].append(sl_s)
        outs["conv_s"].append(conv_s)

    return (xp, xs[:, None, :], jnp.stack(outs["gdn_p"]), jnp.stack(outs["gla_p"]), jnp.stack(outs["conv_p"]),
            jnp.stack(outs["gdn_s"]), jnp.stack(outs["gla_s"]), jnp.stack(outs["conv_s"]))
```

```python
import functools
import itertools

import numpy as np
import jax
import jax.numpy as jnp
from jax import lax
from jax.experimental import pallas as pl
from jax.experimental.pallas import tpu as pltpu

F32 = jnp.float32
BF16 = jnp.bfloat16

D_MODEL = 1024
N_META = 16
CHUNK = 64
HEADS = 4
A_DK = 128
A_DV = 128
A_QK = HEADS * A_DK
A_V = HEADS * A_DV
CONV_W = 4
CONV_CH = 2 * A_QK + A_V
B_DK = 64
B_DV = 128
B_QK = HEADS * B_DK
B_V = HEADS * B_DV
B_GATE_RANK = 16
B_GATE_TAU = 16.0
LOG2E = 1.4426950408889634
LANES = 128
SUBLANES = 8
SMALL_W = LANES
B_QKV = 2 * B_QK + B_V
SCAN_W = CONV_CH + B_QKV + SMALL_W
GATE_W = A_V + B_V + 2 * D_MODEL
VMEM_LIMIT = 56 * 1024 * 1024

_IN_SIZES = (CONV_CH, A_V, HEADS, HEADS, B_QK, B_QK, B_V, B_V, B_GATE_RANK, D_MODEL, D_MODEL)
(_O_QKV, _O_Z, _O_BETA, _O_A, _O_QB, _O_KB, _O_VB, _O_RB, _O_GLR, _O_GA, _O_GB, D_IN) = (
    int(v) for v in np.concatenate([[0], np.cumsum(_IN_SIZES)]))

_BETA0, _G0, _GAM0, _GLR0 = 0, 4, 8, 12


def _dot(a, b):
    return jnp.dot(a, b, preferred_element_type=F32)


def _dot_nt(a, b):
    return lax.dot_general(a, b, (((1,), (1,)), ((), ())), preferred_element_type=F32)


def _dot_tn(a, b):
    return lax.dot_general(a, b, (((0,), (0,)), ((), ())), preferred_element_type=F32)


def _bf(x):
    return x.astype(BF16)


def _tri_cumsum(ltri, x):
    hi = x.astype(BF16)
    r1 = x - hi.astype(F32)
    mid = r1.astype(BF16)
    lo = (r1 - mid.astype(F32)).astype(BF16)
    return _dot(ltri, hi) + _dot(ltri, mid) + _dot(ltri, lo)


def _sigmoid(x):
    return 1.0 / (1.0 + jnp.exp(-x))


def _silu(x):
    return x * _sigmoid(x)


def _softplus(x):
    return jnp.maximum(x, 0.0) + jnp.log1p(jnp.exp(-jnp.abs(x)))


def _log_sigmoid(x):
    return jnp.minimum(x, 0.0) - jnp.log1p(jnp.exp(-jnp.abs(x)))


def _layer_norm(h, g, b):
    mu = jnp.mean(h, axis=-1, keepdims=True)
    d = h - mu
    var = jnp.mean(d * d, axis=-1, keepdims=True)
    return d * lax.rsqrt(var + 1e-5) * g + b


def _layer_spec(arr, l, single_buffer=False):
    kw = {"pipeline_mode": pl.Buffered(1)} if single_buffer else {}
    zeros = (0,) * (arr.ndim - 1)
    return pl.BlockSpec((None,) + arr.shape[1:], lambda *_: (l,) + zeros, **kw)


def _regroup_scan_weights(w_ref, wsc_ref):
    wsc_ref[:, :CONV_CH] = w_ref[:, _O_QKV:_O_Z]
    wsc_ref[:, CONV_CH:CONV_CH + B_QKV] = w_ref[:, _O_QB:_O_RB]
    s0 = CONV_CH + B_QKV
    wsc_ref[:, s0:s0 + SMALL_W] = jnp.zeros((D_MODEL, SMALL_W), BF16)
    wsc_ref[:, s0 + _BETA0:s0 + _GAM0] = w_ref[:, _O_BETA:_O_QB]
    wsc_ref[:, s0 + _GAM0:s0 + _GLR0] = w_ref[:, _O_A:_O_QB]
    wsc_ref[:, s0 + _GLR0:s0 + _GLR0 + B_GATE_RANK] = w_ref[:, _O_GLR:_O_GA]


def _proj_tail(p, act, hp, w2, gb):
    qs, ks = [], []
    for h in range(HEADS):
        qh = act[:, h * A_DK:(h + 1) * A_DK]
        qs.append(qh * lax.rsqrt(jnp.sum(qh * qh, axis=-1, keepdims=True) + 1e-6) * (A_DK ** -0.5))
        kh = act[:, A_QK + h * A_DK:A_QK + (h + 1) * A_DK]
        ks.append(kh * lax.rsqrt(jnp.sum(kh * kh, axis=-1, keepdims=True) + 1e-6))
    qa = jnp.concatenate(qs, axis=-1)
    ka = jnp.concatenate(ks, axis=-1)
    va = act[:, 2 * A_QK:]
    qb = p[:, CONV_CH:CONV_CH + B_QK] * (B_DK ** -0.5)
    kb = p[:, CONV_CH + B_QK:CONV_CH + 2 * B_QK]
    vb = p[:, CONV_CH + 2 * B_QK:CONV_CH + B_QKV]
    small = p[:, CONV_CH + B_QKV:]
    beta = _sigmoid(small)
    g = -(jnp.exp(hp[0:1, :]) * _softplus(small + hp[1:2, :]))
    lg = _log_sigmoid(_dot(_bf(small), w2) + gb) * (1.0 / B_GATE_TAU)
    return qa, ka, va, qb, kb, vb, beta, g, lg


def _proj_prompt_kernel(x_ref, carry0_ref, w_ref, cw_ref, hp_ref, w2_ref, gb_ref, ltri_ref,
                        qa_ref, ka_ref, va_ref, qb_ref, kb_ref, vb_ref, cb_ref, aux_ref, tail_ref,
                        wsc_ref, carry_ref, *, tm, nsub, n_null):
    @pl.when(jnp.logical_and(pl.program_id(0) == 0, pl.program_id(1) == 0))
    def _():
        _regroup_scan_weights(w_ref, wsc_ref)

    @pl.when(pl.program_id(1) == 0)
    def _():
        carry_ref[...] = carry0_ref[...]

    ps = [_dot(_bf(x_ref[0, i * tm:(i + 1) * tm, :]), wsc_ref[...]) for i in range(nsub)]
    prev8 = carry_ref[...]
    cw = cw_ref[...]
    row8 = lax.broadcasted_iota(jnp.int32, (SUBLANES, CONV_CH), 0)
    lane = lax.broadcasted_iota(jnp.int32, (tm, SMALL_W), 1)
    ltri = ltri_ref[...]
    for i, p in enumerate(ps):
        rows = slice(i * tm, (i + 1) * tm)
        u = p[:, :CONV_CH]
        acc = u * cw[CONV_W - 1:CONV_W, :]
        for s in range(1, CONV_W):
            rolled = pltpu.roll(u, s, 0)
            head = jnp.where(row8 < s, pltpu.roll(prev8, s, 0), rolled[:SUBLANES])
            shifted = jnp.concatenate([head, rolled[SUBLANES:]], axis=0)
            acc = acc + shifted * cw[CONV_W - 1 - s:CONV_W - s, :]
        prev8 = u[tm - SUBLANES:, :]

        qa, ka, va, qb, kb, vb, beta, g, lg = _proj_tail(p, _silu(acc), hp_ref[...], w2_ref[...], gb_ref[...])
        if n_null:
            live = lax.broadcasted_iota(jnp.int32, (tm, 1), 0) >= n_null - i * tm
            g = jnp.where(live, g, 0.0)
            lg = jnp.where(live, lg, 0.0)
        gam = _tri_cumsum(ltri, g)
        cb = _tri_cumsum(ltri, lg * LOG2E)
        aux = jnp.where(lane < _G0, beta, jnp.where(lane < _GAM0, g, jnp.where(lane < _GLR0, gam, 0.0)))
        qa_ref[0, rows, :] = qa
        ka_ref[0, rows, :] = ka
        va_ref[0, rows, :] = va
        qb_ref[0, rows, :] = qb
        kb_ref[0, rows, :] = kb
        vb_ref[0, rows, :] = vb
        cb_ref[0, rows, :] = cb
        aux_ref[0, rows, :] = aux
    carry_ref[...] = prev8
    tail_ref[0] = prev8


def _proj_prompt(x, carry0, W, l, *, tm, nsub, n_null):
    nb, length, _ = x.shape
    blk = np.arange(tm) // CHUNK
    ltri = jnp.asarray((blk[:, None] == blk[None, :]) & (np.arange(tm)[:, None] >= np.arange(tm)[None, :]), BF16)
    full = lambda shape: pl.BlockSpec(shape, lambda b, j: (0,) * len(shape))
    row = lambda w: pl.BlockSpec((1, tm * nsub, w), lambda b, j: (b, j, 0))
    out_w = (A_QK, A_QK, A_V, B_QK, B_QK, B_V, B_QK, SMALL_W)
    return pl.pallas_call(
        functools.partial(_proj_prompt_kernel, tm=tm, nsub=nsub, n_null=n_null),
        grid=(nb, length // (tm * nsub)),
        in_specs=[row(D_MODEL), full((SUBLANES, CONV_CH)), _layer_spec(W["w_in"], l, True),
                  _layer_spec(W["conv_w"], l), _layer_spec(W["hp"], l), _layer_spec(W["w2"], l),
                  _layer_spec(W["gb"], l), full((tm, tm))],
        out_specs=[row(w) for w in out_w] + [pl.BlockSpec((1, SUBLANES, CONV_CH), lambda b, j: (b, 0, 0))],
        out_shape=[jax.ShapeDtypeStruct((nb, length, w), F32) for w in out_w]
        + [jax.ShapeDtypeStruct((nb, SUBLANES, CONV_CH), F32)],
        scratch_shapes=[pltpu.VMEM((D_MODEL, SCAN_W), BF16), pltpu.VMEM((SUBLANES, CONV_CH), F32)],
        compiler_params=pltpu.CompilerParams(dimension_semantics=("arbitrary", "arbitrary"),
                                             vmem_limit_bytes=VMEM_LIMIT),
        name="proj_prompt",
    )(x, carry0, W["w_in"], W["conv_w"], W["hp"], W["w2"], W["gb"], ltri)


def _proj_decode_kernel(x_ref, st_ref, w_ref, cw_ref, hp_ref, w2_ref, gb_ref,
                        qa_ref, ka_ref, va_ref, qb_ref, kb_ref, vb_ref, lg_ref, aux_ref, stn_ref, wsc_ref):
    _regroup_scan_weights(w_ref, wsc_ref)
    p = _dot(_bf(x_ref[...]), wsc_ref[...])
    u = p[:, :CONV_CH]
    cw = cw_ref[...]
    st1 = st_ref[:, 1, :]
    st2 = st_ref[:, 2, :]
    acc = st_ref[:, 0, :] * cw[0:1, :] + st1 * cw[1:2, :] + st2 * cw[2:3, :] + u * cw[3:4, :]
    qa, ka, va, qb, kb, vb, beta, g, lg = _proj_tail(p, _silu(acc), hp_ref[...], w2_ref[...], gb_ref[...])
    lane = lax.broadcasted_iota(jnp.int32, beta.shape, 1)
    qa_ref[...] = qa
    ka_ref[...] = ka
    va_ref[...] = va
    qb_ref[...] = qb
    kb_ref[...] = kb
    vb_ref[...] = vb
    lg_ref[...] = lg
    aux_ref[...] = jnp.where(lane < _G0, beta, jnp.where(lane < _GAM0, g, 0.0))
    stn_ref[:, 0, :] = st1
    stn_ref[:, 1, :] = st2
    stn_ref[:, 2, :] = u


def _proj_decode(x, state_conv, W, l):
    n = x.shape[0]
    out_w = (A_QK, A_QK, A_V, B_QK, B_QK, B_V, B_QK, SMALL_W)
    whole = lambda shape: pl.BlockSpec(shape, lambda i: (0,) * len(shape))
    return pl.pallas_call(
        _proj_decode_kernel,
        grid=(1,),
        in_specs=[whole((n, D_MODEL)), _layer_spec(state_conv, l), _layer_spec(W["w_in"], l, True),
                  _layer_spec(W["conv_w"], l), _layer_spec(W["hp"], l), _layer_spec(W["w2"], l),
                  _layer_spec(W["gb"], l)],
        out_specs=[whole((n, w)) for w in out_w] + [whole((n, CONV_W - 1, CONV_CH))],
        out_shape=[jax.ShapeDtypeStruct((n, w), F32) for w in out_w]
        + [jax.ShapeDtypeStruct((n, CONV_W - 1, CONV_CH), F32)],
        scratch_shapes=[pltpu.VMEM((D_MODEL, SCAN_W), BF16)],
        compiler_params=pltpu.CompilerParams(dimension_semantics=("arbitrary",), vmem_limit_bytes=VMEM_LIMIT),
        name="proj_decode",
    )(x, state_conv, W["w_in"], W["conv_w"], W["hp"], W["w2"], W["gb"])


def _gdn_kernel(q_ref, k_ref, v_ref, aux_ref, gamt_ref, s0_ref, o_ref, sfin_ref, s_ref, *, bt):
    c = pl.program_id(1)

    @pl.when(c == 0)
    def _():
        for b in range(bt):
            s_ref[b] = s0_ref[0]

    row = lax.broadcasted_iota(jnp.int32, (CHUNK, LANES), 0)
    lane = lax.broadcasted_iota(jnp.int32, (CHUNK, LANES), 1)
    col = lane % CHUNK
    low = lane < CHUNK
    eye = (row == col).astype(F32)
    zk = jnp.zeros((CHUNK, A_DK), BF16)
    zs = jnp.zeros((A_DK, A_DV), BF16)

    def block_diag(m0, m1, z):
        return jnp.concatenate([jnp.concatenate([m0, z], axis=-1), jnp.concatenate([z, m1], axis=-1)], axis=0)

    def pair_diag(y):
        return _bf(jnp.concatenate([jnp.where(low, y, 0.0), jnp.where(low, 0.0, y)], axis=0))

    chains = list(itertools.product(range(bt), range(HEADS // 2)))
    n = len(chains)
    two = lambda p: slice(2 * p * A_DK, (2 * p + 2) * A_DK)
    hcol = lambda b, h, off: aux_ref[b, :, off + h:off + h + 1]
    pcol = lambda b, p, off: jnp.where(low, hcol(b, 2 * p, off), hcol(b, 2 * p + 1, off))

    eye2 = (lax.broadcasted_iota(jnp.int32, (2 * A_DK, 2 * A_DK), 0)
            == lax.broadcasted_iota(jnp.int32, (2 * A_DK, 2 * A_DK), 1)).astype(BF16)
    glast = [[gamt_ref[b, 0, p:p + 1, (i + 1) * CHUNK - 1:(i + 1) * CHUNK] for i in range(2)] for b, p in chains]
    kdt = [_bf(_dot_nt(eye2, _bf(jnp.concatenate(
        [k_ref[b, :, (2 * p + i) * A_DK:(2 * p + i + 1) * A_DK] * jnp.exp(glast[g][i] - hcol(b, 2 * p + i, _GAM0))
         for i in range(2)], axis=-1)))) for g, (b, p) in enumerate(chains)]
    qkk, dec, a = [], [], []
    for b, p in chains:
        k2 = _bf(k_ref[b, :, two(p)])
        lhs = jnp.concatenate([_bf(q_ref[b, :, two(p)]), k2], axis=0)
        qkk.append(_dot_nt(lhs, block_diag(k2[:, :A_DK], k2[:, A_DK:], zk)))
    for g, (b, p) in enumerate(chains):
        dec.append(jnp.exp(jnp.where(row >= col, pcol(b, p, _GAM0) - gamt_ref[b, 0, p:p + 1, :], -jnp.inf)))
        a.append(jnp.where(row > col, pcol(b, p, _BETA0) * dec[g] * qkk[g][CHUNK:], 0.0))
    x = [eye - m for m in a]
    y = [_dot(_bf(m), pair_diag(m)) for m in a]
    steps = int(np.log2(CHUNK)) - 1
    for i in range(steps):
        yd = [pair_diag(m) for m in y]
        if i + 1 < steps:
            xy = [_dot(_bf(jnp.concatenate([x[g], y[g]], axis=0)), yd[g]) for g in range(n)]
            x = [x[g] + xy[g][:CHUNK] for g in range(n)]
            y = [xy[g][CHUNK:] for g in range(n)]
        else:
            x = [x[g] + _dot(_bf(x[g]), yd[g]) for g in range(n)]

    def head_rhs(b, h):
        lanes = slice(h * A_DK, (h + 1) * A_DK)
        beta = hcol(b, h, _BETA0)
        return _bf(jnp.concatenate([beta * v_ref[b, :, lanes],
                                    (beta * jnp.exp(hcol(b, h, _GAM0))) * k_ref[b, :, lanes]], axis=-1))

    z2 = jnp.zeros((CHUNK, A_DV + A_DK), BF16)
    sol = [_dot(_bf(x[g]), block_diag(head_rhs(b, 2 * p), head_rhs(b, 2 * p + 1), z2))
           for g, (b, p) in enumerate(chains)]
    ws = []
    for g, (b, p) in enumerate(chains):
        w2 = jnp.concatenate([sol[g][:, A_DV:A_DV + A_DK], sol[g][:, 2 * A_DV + A_DK:]], axis=-1)
        qd = jnp.concatenate([q_ref[b, :, (2 * p + i) * A_DK:(2 * p + i + 1) * A_DK]
                              * jnp.exp(hcol(b, 2 * p + i, _GAM0)) for i in range(2)], axis=-1)
        s2 = block_diag(_bf(s_ref[b, 2 * p]), _bf(s_ref[b, 2 * p + 1]), zs)
        ws.append(_dot(_bf(jnp.concatenate([w2, qd], axis=0)), s2))
    u = [_bf(jnp.concatenate([sol[g][:, :A_DV], sol[g][:, A_DV + A_DK:2 * A_DV + A_DK]], axis=-1) - ws[g][:CHUNK])
         for g in range(n)]
    zu = jnp.zeros((CHUNK, A_DV), BF16)
    for g, (b, p) in enumerate(chains):
        u2 = block_diag(u[g][:, :A_DV], u[g][:, A_DV:], zu)
        o_ref[b, :, two(p)] = ws[g][CHUNK:] + _dot(_bf(qkk[g][:CHUNK] * dec[g]), u2)
    for g, (b, p) in enumerate(chains):
        upd = _dot(kdt[g], u[g])
        s_ref[b, 2 * p] = jnp.exp(glast[g][0]) * s_ref[b, 2 * p] + upd[:A_DK, :A_DV]
        s_ref[b, 2 * p + 1] = jnp.exp(glast[g][1]) * s_ref[b, 2 * p + 1] + upd[A_DK:, A_DV:]

    @pl.when(c == pl.num_programs(1) - 1)
    def _():
        sfin_ref[...] = s_ref[...]


def _gdn_scan(qa, ka, va, aux, s0, *, bt):
    nb, length, _ = qa.shape
    nc = length // CHUNK
    gamt = (aux[:, :, _GAM0:_GAM0 + HEADS].reshape(nb, nc, CHUNK, HEADS).transpose(0, 1, 3, 2)
            .reshape(nb, nc, HEADS // 2, 2 * CHUNK))
    row = lambda w: pl.BlockSpec((bt, CHUNK, w), lambda b, c: (b, c, 0))
    return pl.pallas_call(
        functools.partial(_gdn_kernel, bt=bt),
        grid=(nb // bt, nc),
        in_specs=[row(A_QK), row(A_QK), row(A_V), row(SMALL_W),
                  pl.BlockSpec((bt, 1, HEADS // 2, 2 * CHUNK), lambda b, c: (b, c, 0, 0)),
                  pl.BlockSpec((1, HEADS, A_DK, A_DV), lambda b, c: (0, 0, 0, 0))],
        out_specs=[row(A_V), pl.BlockSpec((bt, HEADS, A_DK, A_DV), lambda b, c: (b, 0, 0, 0))],
        out_shape=[jax.ShapeDtypeStruct((nb, length, A_V), F32),
                   jax.ShapeDtypeStruct((nb, HEADS, A_DK, A_DV), F32)],
        scratch_shapes=[pltpu.VMEM((bt, HEADS, A_DK, A_DV), F32)],
        compiler_params=pltpu.CompilerParams(dimension_semantics=("arbitrary", "arbitrary")),
        name="gdn_scan",
    )(qa, ka, va, aux, gamt, s0)


def _gla_kernel(q_ref, k_ref, c_ref, v_ref, seg_ref, s0_ref, o_ref, sfin_ref, s_ref, *, bt):
    c = pl.program_id(1)

    @pl.when(c == 0)
    def _():
        for b in range(bt):
            s_ref[b] = s0_ref[0]

    row = lax.broadcasted_iota(jnp.int32, (CHUNK, LANES), 0)
    lane = lax.broadcasted_iota(jnp.int32, (CHUNK, LANES), 1)
    col = lane % B_DK
    low = lane < B_DK
    levels = (32, 16, SUBLANES)
    level_mask = [jnp.logical_and(row // (2 * s) == col // (2 * s),
                                  jnp.logical_and((row // s) % 2 == 1, (col // s) % 2 == 0)) for s in levels]
    diag_mask = jnp.logical_and(row // SUBLANES == col // SUBLANES, col % SUBLANES <= row % SUBLANES)
    zeros_v = jnp.zeros((CHUNK, B_DV), BF16)
    zeros_s = jnp.zeros((B_DK, B_DV), BF16)
    nblk = CHUNK // SUBLANES

    pairs = list(itertools.product(range(bt), range(HEADS // 2)))
    lanes_of = lambda p: slice(p * LANES, (p + 1) * LANES)
    q = [q_ref[b, :, lanes_of(p)] for b, p in pairs]
    k = [k_ref[b, :, lanes_of(p)] for b, p in pairs]
    cb = [c_ref[b, :, lanes_of(p)] for b, p in pairs]

    zs = []
    for g, (b, p) in enumerate(pairs):
        pieces = []
        for jj in range(SUBLANES):
            bcast = lambda ref: jnp.concatenate(
                [jnp.broadcast_to(ref[b, SUBLANES * i + jj:SUBLANES * i + jj + 1, lanes_of(p)], (SUBLANES, LANES))
                 for i in range(nblk)], axis=0)
            pieces.append(q[g] * bcast(k_ref) * jnp.exp2(jnp.minimum(cb[g] - bcast(c_ref), 0.0)))
        zs.append(_bf(jnp.concatenate(pieces, axis=-1)))
    diag = _dot(jnp.concatenate(zs, axis=0), seg_ref[...])

    scores = []
    for s in levels:
        lvl = []
        for g in range(len(pairs)):
            cs = cb[g].reshape(CHUNK // s, s, LANES)[:, 0:1, :]
            start = jnp.broadcast_to(cs, (CHUNK // s, s, LANES)).reshape(CHUNK, LANES)
            nxt = jnp.broadcast_to(jnp.concatenate([cs[1:], cs[-1:]], axis=0),
                                   (CHUNK // s, s, LANES)).reshape(CHUNK, LANES)
            qt = _bf(q[g] * jnp.exp2(cb[g] - start))
            kt = k[g] * jnp.exp2(jnp.minimum(nxt - cb[g], 0.0))
            kt2 = _bf(jnp.concatenate([jnp.where(low, kt, 0.0), jnp.where(low, 0.0, kt)], axis=0))
            lvl.append(_dot_nt(qt, kt2))
        scores.append(lvl)

    for g, (b, p) in enumerate(pairs):
        attn = jnp.where(diag_mask, diag[g * CHUNK:(g + 1) * CHUNK], 0.0)
        for m, lvl in zip(level_mask, scores):
            attn = jnp.where(m, lvl[g], attn)
        v0 = _bf(v_ref[b, :, (2 * p) * B_DV:(2 * p + 1) * B_DV])
        v1 = _bf(v_ref[b, :, (2 * p + 1) * B_DV:(2 * p + 2) * B_DV])
        s0 = s_ref[b, 2 * p]
        s1 = s_ref[b, 2 * p + 1]
        lhs = jnp.concatenate([_bf(q[g] * jnp.exp2(cb[g])), _bf(attn)], axis=-1)
        rhs = jnp.concatenate([jnp.concatenate([_bf(s0), zeros_s], axis=-1),
                               jnp.concatenate([zeros_s, _bf(s1)], axis=-1),
                               jnp.concatenate([v0, zeros_v], axis=-1),
                               jnp.concatenate([zeros_v, v1], axis=-1)], axis=0)
        o_ref[b, :, (2 * p) * B_DV:(2 * p + 2) * B_DV] = _dot(lhs, rhs)
        ct = cb[g].T
        clast = ct[:, CHUNK - 1:CHUNK]
        upd = _dot(_bf(k[g].T * jnp.exp2(clast - ct)), jnp.concatenate([v0, v1], axis=-1))
        decay = jnp.exp2(clast)
        s_ref[b, 2 * p] = decay[:B_DK] * s0 + upd[:B_DK, :B_DV]
        s_ref[b, 2 * p + 1] = decay[B_DK:] * s1 + upd[B_DK:, B_DV:]

    @pl.when(c == pl.num_programs(1) - 1)
    def _():
        sfin_ref[...] = s_ref[...]


def _gla_scan(qb, kb, cb, vb, s0, *, bt):
    nb, length, _ = qb.shape
    nc = length // CHUNK
    rows = np.arange(SUBLANES * LANES)
    cols = np.arange(LANES)
    seg = jnp.asarray(((rows[:, None] % LANES) // B_DK == cols[None, :] // B_DK)
                      & (rows[:, None] // LANES == cols[None, :] % SUBLANES), BF16)
    row = lambda w: pl.BlockSpec((bt, CHUNK, w), lambda b, c: (b, c, 0))
    return pl.pallas_call(
        functools.partial(_gla_kernel, bt=bt),
        grid=(nb // bt, nc),
        in_specs=[row(B_QK), row(B_QK), row(B_QK), row(B_V),
                  pl.BlockSpec(seg.shape, lambda b, c: (0, 0)),
                  pl.BlockSpec((1, HEADS, B_DK, B_DV), lambda b, c: (0, 0, 0, 0))],
        out_specs=[row(B_V), pl.BlockSpec((bt, HEADS, B_DK, B_DV), lambda b, c: (b, 0, 0, 0))],
        out_shape=[jax.ShapeDtypeStruct((nb, length, B_V), F32),
                   jax.ShapeDtypeStruct((nb, HEADS, B_DK, B_DV), F32)],
        scratch_shapes=[pltpu.VMEM((bt, HEADS, B_DK, B_DV), F32)],
        compiler_params=pltpu.CompilerParams(dimension_semantics=("arbitrary", "arbitrary")),
        name="gla_scan",
    )(qb, kb, cb, vb, seg, s0)


def _decode_kernel(qa_ref, ka_ref, va_ref, aux_ref, qb_ref, kb_ref, vb_ref, lg_ref, sg_ref, sl_ref,
                   oa_ref, ob_ref, sgn_ref, sln_ref, *, bt):
    qat = qa_ref[...].T
    kat = ka_ref[...].T
    qbt = qb_ref[...].T
    kbt = kb_ref[...].T
    decay_t = jnp.exp(lg_ref[...]).T
    for b in range(bt):
        for h in range(HEADS):
            s = sg_ref[b, h]
            kcol = kat[h * A_DK:(h + 1) * A_DK, b:b + 1]
            qcol = qat[h * A_DK:(h + 1) * A_DK, b:b + 1]
            beta = aux_ref[b:b + 1, _BETA0 + h:_BETA0 + h + 1]
            eg = jnp.exp(aux_ref[b:b + 1, _G0 + h:_G0 + h + 1])
            r = jnp.sum(kcol * s, axis=0, keepdims=True)
            u = beta * (va_ref[b:b + 1, h * A_DV:(h + 1) * A_DV] - eg * r)
            sn = eg * s + kcol * u
            sgn_ref[b, h] = sn
            oa_ref[b:b + 1, h * A_DV:(h + 1) * A_DV] = jnp.sum(qcol * sn, axis=0, keepdims=True)

            s = sl_ref[b, h]
            kcol = kbt[h * B_DK:(h + 1) * B_DK, b:b + 1]
            qcol = qbt[h * B_DK:(h + 1) * B_DK, b:b + 1]
            dcol = decay_t[h * B_DK:(h + 1) * B_DK, b:b + 1]
            sn = dcol * s + kcol * vb_ref[b:b + 1, h * B_DV:(h + 1) * B_DV]
            sln_ref[b, h] = sn
            ob_ref[b:b + 1, h * B_DV:(h + 1) * B_DV] = jnp.sum(qcol * sn, axis=0, keepdims=True)


def _decode_mix(qa, ka, va, aux, qb, kb, vb, lg, state_gdn, state_gla, l, *, bt=8):
    n = qa.shape[0]
    row = lambda w: pl.BlockSpec((bt, w), lambda i: (i, 0))
    sg_in = pl.BlockSpec((None, bt, HEADS, A_DK, A_DV), lambda i: (l, i, 0, 0, 0))
    sl_in = pl.BlockSpec((None, bt, HEADS, B_DK, B_DV), lambda i: (l, i, 0, 0, 0))
    sgs = pl.BlockSpec((bt, HEADS, A_DK, A_DV), lambda i: (i, 0, 0, 0))
    sls = pl.BlockSpec((bt, HEADS, B_DK, B_DV), lambda i: (i, 0, 0, 0))
    return pl.pallas_call(
        functools.partial(_decode_kernel, bt=bt),
        grid=(n // bt,),
        in_specs=[row(A_QK), row(A_QK), row(A_V), row(SMALL_W), row(B_QK), row(B_QK), row(B_V), row(B_QK),
                  sg_in, sl_in],
        out_specs=[row(A_V), row(B_V), sgs, sls],
        out_shape=[jax.ShapeDtypeStruct((n, A_V), F32), jax.ShapeDtypeStruct((n, B_V), F32),
                   jax.ShapeDtypeStruct(state_gdn.shape[1:], F32), jax.ShapeDtypeStruct(state_gla.shape[1:], F32)],
        compiler_params=pltpu.CompilerParams(dimension_semantics=("arbitrary",)),
        name="decode_mix",
    )(qa, ka, va, aux, qb, kb, vb, lg, state_gdn, state_gla)


def _head_rms(o, dv):
    parts = []
    for h in range(HEADS):
        oh = o[:, h * dv:(h + 1) * dv]
        parts.append(oh * lax.rsqrt(jnp.mean(oh * oh, axis=-1, keepdims=True) + 1e-6))
    return jnp.concatenate(parts, axis=-1)


def _mix_kernel(x_ref, oa_ref, ob_ref, w_ref, na_ref, nb_ref, wa_ref, wb_ref, wo_ref, g_ref, b_ref, out_ref,
                wg_ref, *, tm, nsub, alpha):
    @pl.when(pl.program_id(0) == 0)
    def _():
        wg_ref[:, :A_V] = w_ref[:, _O_Z:_O_BETA]
        wg_ref[:, A_V:A_V + B_V] = w_ref[:, _O_RB:_O_GLR]
        wg_ref[:, A_V + B_V:] = w_ref[:, _O_GA:D_IN]

    subs = [slice(i * tm, (i + 1) * tm) for i in range(nsub)]
    gates = [_dot(_bf(x_ref[r, :]), wg_ref[...]) for r in subs]
    ys = []
    for r, gt in zip(subs, gates):
        oa = _head_rms(oa_ref[r, :], A_DV) * na_ref[...] * _silu(gt[:, :A_V])
        ob = _head_rms(ob_ref[r, :], B_DV) * nb_ref[...] * _silu(gt[:, A_V:A_V + B_V])
        ys.append((_dot(_bf(oa), wa_ref[...]), _dot(_bf(ob), wb_ref[...])))
    mixes = []
    for gt, (ya, yb) in zip(gates, ys):
        merged = (_sigmoid(gt[:, A_V + B_V:A_V + B_V + D_MODEL]) * ya
                  + _sigmoid(gt[:, A_V + B_V + D_MODEL:]) * yb)
        mixes.append(_dot(_bf(merged), wo_ref[...]))
    for r, mix in zip(subs, mixes):
        out_ref[r, :] = _layer_norm(alpha * x_ref[r, :] + mix, g_ref[...], b_ref[...])


def _mix(x, oa, ob, W, l, *, tm, nsub, alpha):
    n = x.shape[0]
    row = lambda w: pl.BlockSpec((tm * nsub, w), lambda i: (i, 0))
    names = ("w_in", "na", "nb", "wa", "wb", "wo", "ln1g", "ln1b")
    return pl.pallas_call(
        functools.partial(_mix_kernel, tm=tm, nsub=nsub, alpha=alpha),
        grid=(n // (tm * nsub),),
        in_specs=[row(D_MODEL), row(A_V), row(B_V)] + [_layer_spec(W[k], l, W[k].dtype == BF16) for k in names],
        out_specs=row(D_MODEL),
        out_shape=jax.ShapeDtypeStruct((n, D_MODEL), F32),
        scratch_shapes=[pltpu.VMEM((D_MODEL, GATE_W), BF16)],
        compiler_params=pltpu.CompilerParams(dimension_semantics=("arbitrary",), vmem_limit_bytes=VMEM_LIMIT),
        name="mix",
    )(x, oa, ob, *[W[k] for k in names])


def _ffn_kernel(x_ref, wi_ref, wo_ref, g_ref, b_ref, out_ref, *, alpha, d_ff, fc):
    x = x_ref[...]
    xb = _bf(x)
    acc = jnp.zeros(x.shape, F32)
    for c in range(d_ff // fc):
        a = _dot(xb, wi_ref[:, c * fc:(c + 1) * fc])
        u = _dot(xb, wi_ref[:, d_ff + c * fc:d_ff + (c + 1) * fc])
        acc = acc + _dot(_bf(_silu(a) * u), wo_ref[c * fc:(c + 1) * fc, :])
    out_ref[...] = _layer_norm(alpha * x + acc, g_ref[...], b_ref[...])


def _ffn(x, W, l, *, tm, alpha):
    n = x.shape[0]
    d_ff = W["wfo"].shape[1]
    row = pl.BlockSpec((tm, D_MODEL), lambda i: (i, 0))
    names = ("wfi", "wfo", "ln2g", "ln2b")
    return pl.pallas_call(
        functools.partial(_ffn_kernel, alpha=alpha, d_ff=d_ff, fc=256),
        grid=(n // tm,),
        in_specs=[row] + [_layer_spec(W[k], l, W[k].dtype == BF16) for k in names],
        out_specs=row,
        out_shape=jax.ShapeDtypeStruct((n, D_MODEL), F32),
        compiler_params=pltpu.CompilerParams(dimension_semantics=("arbitrary",), vmem_limit_bytes=VMEM_LIMIT),
        name="ffn",
    )(x, *[W[k] for k in names])


def _stacked_params(w_in, conv_w, a_log, dt_bias, gdn_norm_w, gla_gate_w2, gla_gate_b, gla_norm_w,
                    w_branch_a, w_branch_b, w_out, ln1_g, ln1_b, ln2_g, ln2_b, w_ffn_in, w_ffn_out):
    lane_row = lambda v: jnp.pad(jnp.concatenate([v, v], axis=-1), ((0, 0), (_G0, SMALL_W - _GLR0)))
    hp = jnp.pad(jnp.stack([lane_row(a_log), lane_row(dt_bias)], axis=1), ((0, 0), (0, SUBLANES - 2), (0, 0)))
    w2 = jnp.pad(gla_gate_w2, ((0, 0), (_GLR0, SMALL_W - _GLR0 - B_GATE_RANK), (0, 0)))
    return {
        "w_in": _bf(w_in), "conv_w": conv_w, "hp": hp, "w2": _bf(w2), "gb": gla_gate_b[:, None, :],
        "na": jnp.tile(gdn_norm_w, (1, HEADS))[:, None, :], "nb": jnp.tile(gla_norm_w, (1, HEADS))[:, None, :],
        "wa": _bf(w_branch_a), "wb": _bf(w_branch_b), "wo": _bf(w_out),
        "ln1g": ln1_g[:, None, :], "ln1b": ln1_b[:, None, :], "ln2g": ln2_g[:, None, :], "ln2b": ln2_b[:, None, :],
        "wfi": _bf(w_ffn_in), "wfo": _bf(w_ffn_out),
    }


def kernel(x_prompt, x_sample, state_gdn, state_gla, state_conv, meta_tokens, w_in, conv_w, a_log, dt_bias,
           gdn_norm_w, gla_gate_w2, gla_gate_b, gla_norm_w, w_branch_a, w_branch_b, w_out,
           ln1_g, ln1_b, ln2_g, ln2_b, w_ffn_in, w_ffn_out):
    depth = w_in.shape[0]
    alpha = (2.0 * depth) ** 0.25
    nb, seq, _ = x_prompt.shape
    assert seq % CHUNK == 0 and x_sample.shape[1] == 1 and meta_tokens.shape[0] == N_META
    assert w_in.shape[2] == D_IN
    tm_proj = 256
    nsub_proj = 4
    tm_mix = 256
    nsub_mix = 2
    tm_ffn = 512
    bt_scan = nb
    null_rows = CHUNK - N_META
    W = _stacked_params(w_in, conv_w, a_log, dt_bias, gdn_norm_w, gla_gate_w2, gla_gate_b, gla_norm_w,
                        w_branch_a, w_branch_b, w_out, ln1_g, ln1_b, ln2_g, ln2_b, w_ffn_in, w_ffn_out)

    xp = x_prompt
    xs = x_sample[:, 0, :]
    ns = xs.shape[0]
    xm = meta_tokens
    zeros_rows = jnp.zeros((null_rows, D_MODEL), F32)
    zero_tail = jnp.zeros((SUBLANES, CONV_CH), F32)
    zero_gdn = jnp.zeros((1, HEADS, A_DK, A_DV), F32)
    zero_gla = jnp.zeros((1, HEADS, B_DK, B_DV), F32)
    outs = {k: [] for k in ("gdn_p", "gla_p", "conv_p", "gdn_s", "gla_s", "conv_s")}
    for l in range(depth):
        xm_pad = jnp.concatenate([zeros_rows, xm], axis=0)[None]
        qa, ka, va, qb, kb, vb, cb, aux, tail_m = _proj_prompt(xm_pad, zero_tail, W, l, tm=CHUNK, nsub=1, n_null=null_rows)
        oa_m, sg_m = _gdn_scan(qa, ka, va, aux, zero_gdn, bt=1)
        ob_m, sl_m = _gla_scan(qb, kb, cb, vb, zero_gla, bt=1)

        qa, ka, va, qb, kb, vb, cb, aux, tail_p = _proj_prompt(xp, tail_m[0], W, l, tm=tm_proj, nsub=nsub_proj, n_null=0)
        oa_p, sg_p = _gdn_scan(qa, ka, va, aux, sg_m, bt=bt_scan)
        ob_p, sl_p = _gla_scan(qb, kb, cb, vb, sl_m, bt=bt_scan)
        x1 = _mix(xp.reshape(nb * seq, D_MODEL), oa_p.reshape(nb * seq, A_V), ob_p.reshape(nb * seq, B_V), W, l,
                  tm=tm_mix, nsub=nsub_mix, alpha=alpha)
        xp = _ffn(x1, W, l, tm=tm_ffn, alpha=alpha).reshape(nb, seq, D_MODEL)
        outs["gdn_p"].append(sg_p)
        outs["gla_p"].append(sl_p)
        outs["conv_p"].append(tail_p[:, SUBLANES - (CONV_W - 1):, :])

        qa, ka, va, qb, kb, vb, lg, aux, conv_s = _proj_decode(xs, state_conv, W, l)
        oa_s, ob_s, sg_s, sl_s = _decode_mix(qa, ka, va, aux, qb, kb, vb, lg, state_gdn, state_gla, l)

        if l + 1 < depth:
            x_small = jnp.concatenate([xs, xm], axis=0)
            oa_small = jnp.concatenate([oa_s, oa_m[0, null_rows:]], axis=0)
            ob_small = jnp.concatenate([ob_s, ob_m[0, null_rows:]], axis=0)
        else:
            x_small, oa_small, ob_small = xs, oa_s, ob_s
        rows = x_small.shape[0]
        x_small = _ffn(_mix(x_small, oa_small, ob_small, W, l, tm=rows, nsub=1, alpha=alpha), W, l, tm=rows,
                       alpha=alpha)
        xs, xm = x_small[:ns], x_small[ns:]
        outs["gdn_s"].append(sg_s)
        outs["gla_s"].append(sl_s)
        outs["conv_s"].append(conv_s)

    return (xp, xs[:, None, :], jnp.stack(outs["gdn_p"]), jnp.stack(outs["gla_p"]), jnp.stack(outs["conv_p"]),
            jnp.stack(outs["gdn_s"]), jnp.stack(outs["gla_s"]), jnp.stack(outs["conv_s"]))
```
